```python
import functools
import jax, jax.numpy as jnp
from jax import lax
import numpy as np

D_MODEL = 1024
BATCH = 8
SEQ = 4096
DEPTH = 2
DEC_BATCH = 32
DEC_SEQ = 1
PAST_LEN = 16384
PAGE_SIZE = 128

CHUNK = 128
A_GROUPS = 4
A_GROUP_DIM = 128
A_WIDTH = A_GROUPS * A_GROUP_DIM
N_HEADS = 8
N_KV_HEADS = 4
HEAD_DIM = 128
GQA_GROUP = N_HEADS // N_KV_HEADS
IDX_HEADS = 8
IDX_DIM = 64
TOPK_MAX = 256
Q_BLOCK = 128
ROPE_THETA = 10000.0
MEM_TOKENS = 256
MEM_HEADS = 4
MEM_HEAD_DIM = 128
D_FF = 2816
CONV_WIDTH = 3
N_NORMS = 6
RMS_EPS = 1e-6
LN_EPS = 1e-5
IN_SIZES = (A_WIDTH, A_WIDTH, N_HEADS * HEAD_DIM, N_KV_HEADS * HEAD_DIM, N_KV_HEADS * HEAD_DIM,
            IDX_HEADS * IDX_DIM, IDX_DIM, IDX_HEADS, D_MODEL, D_MODEL)
IN_WIDTH = sum(IN_SIZES)

kernel_name = "hybrid_gmlp_dsa_decoder_step"


def rms_norm(x, g):
    xf = x.astype(jnp.float32)
    y = xf * lax.rsqrt(jnp.mean(xf * xf, axis=-1, keepdims=True) + RMS_EPS)
    return (y * g.astype(jnp.float32)).astype(x.dtype)


def layer_norm(x, g, b):
    xf = x.astype(jnp.float32)
    mu = jnp.mean(xf, axis=-1, keepdims=True)
    xc = xf - mu
    var = jnp.mean(xc * xc, axis=-1, keepdims=True)
    return (xc * lax.rsqrt(var + LN_EPS) * g.astype(jnp.float32) + b.astype(jnp.float32)).astype(x.dtype)


def rope(x, pos):
    half = x.shape[-1] // 2
    inv_freq = ROPE_THETA ** (-jnp.arange(half, dtype=jnp.float32) / half)
    ang = pos.astype(jnp.float32)[:, None] * inv_freq[None, :]
    cos = jnp.cos(ang)[None, :, None, :]
    sin = jnp.sin(ang)[None, :, None, :]
    xf = x.astype(jnp.float32)
    x1, x2 = xf[..., :half], xf[..., half:]
    return jnp.concatenate([x1 * cos - x2 * sin, x1 * sin + x2 * cos], axis=-1).astype(x.dtype)


def chunk_spatial_gate(u, v, w_spatial, b_spatial):
    B, S, _ = u.shape
    n = -(-S // CHUNK) * CHUNK
    vp = jnp.pad(v, ((0, 0), (0, n - S), (0, 0)))
    vc = vp.reshape(B, n // CHUNK, CHUNK, A_GROUPS, A_GROUP_DIM)
    w = w_spatial * jnp.tril(jnp.ones((CHUNK, CHUNK), w_spatial.dtype))[None]
    sv = jnp.einsum('gts,bcsgd->bctgd', w, vc) + b_spatial.T[None, None, :, :, None]
    sv = sv.reshape(B, n, A_WIDTH)[:, :S]
    return u * sv


def index_scores(qi, ki, wi):
    logits = jnp.einsum('bthd,bsd->bths', qi, ki).astype(jnp.float32) * (IDX_DIM ** -0.5)
    return jnp.einsum('bth,bths->bts', wi.astype(jnp.float32), jax.nn.relu(logits))


def sparse_attend(q, kg, vg, valid):
    B, T = q.shape[:2]
    qg = q.reshape(B, T, N_KV_HEADS, GQA_GROUP, HEAD_DIM)
    s = jnp.einsum('btkgd,btnkd->btkgn', qg, kg).astype(jnp.float32) * (HEAD_DIM ** -0.5)
    s = jnp.where(valid[:, :, None, None, :], s, -jnp.inf)
    p = jax.nn.softmax(s, axis=-1).astype(vg.dtype)
    o = jnp.einsum('btkgn,btnkd->btkgd', p, vg)
    return o.reshape(B, T, N_HEADS * HEAD_DIM)


def take_rows(arr, idx):
    return jax.vmap(lambda a, i: a[i])(arr, idx)


def dsa_prompt(q, k, v, qi, ki, wi):
    B, S = q.shape[:2]
    topk = min(TOPK_MAX, S // 4)
    key_pos = jnp.arange(S)

    def block(start):
        qb = lax.dynamic_slice_in_dim(q, start, Q_BLOCK, axis=1)
        qib = lax.dynamic_slice_in_dim(qi, start, Q_BLOCK, axis=1)
        wib = lax.dynamic_slice_in_dim(wi, start, Q_BLOCK, axis=1)
        qpos = start + jnp.arange(Q_BLOCK)
        score = index_scores(qib, ki, wib)
        admissible = key_pos[None, :] <= qpos[:, None]
        score = jnp.where(admissible[None], score, -jnp.inf)
        _, idx = lax.top_k(score, topk)
        valid = idx <= qpos[None, :, None]
        return sparse_attend(qb, take_rows(k, idx), take_rows(v, idx), valid)

    out = lax.map(block, jnp.arange(0, S, Q_BLOCK))
    return jnp.swapaxes(out, 0, 1).reshape(B, S, N_HEADS * HEAD_DIM)


def dsa_sample(q, k, v, qi, ki, wi, cache_k, cache_v, cache_idx_k, page_table, layer):
    B, T = q.shape[:2]
    past = page_table.shape[1] * PAGE_SIZE
    L = past + T
    topk = min(TOPK_MAX, L // 4)
    ki_past = cache_idx_k[layer, page_table].reshape(B, past, IDX_DIM)
    ki_all = jnp.concatenate([ki_past, ki.astype(ki_past.dtype)], axis=1)
    score = index_scores(qi, ki_all, wi)
    qpos = past + jnp.arange(T)
    admissible = jnp.arange(L)[None, :] <= qpos[:, None]
    score = jnp.where(admissible[None], score, -jnp.inf)
    _, idx = lax.top_k(score, topk)
    valid = idx <= qpos[None, :, None]
    in_past = (idx < past)[..., None, None]
    pidx = jnp.minimum(idx, past - 1)
    phys = take_rows(page_table, pidx // PAGE_SIZE)
    off = pidx % PAGE_SIZE
    nidx = jnp.clip(idx - past, 0, T - 1)
    kg = jnp.where(in_past, cache_k[layer, phys, off], take_rows(k, nidx).astype(cache_k.dtype))
    vg = jnp.where(in_past, cache_v[layer, phys, off], take_rows(v, nidx).astype(cache_v.dtype))
    return sparse_attend(q, kg, vg, valid)


def parallel_mixer(h, pos, attend, w_in, ln_v_g, ln_v_b, w_spatial, b_spatial, w_branch_a, w_branch_b, w_out):
    B, S, _ = h.shape
    z = h @ w_in
    u, v, q, k, vv, qi, ki, wi, ga, gb = jnp.split(z, np.cumsum(IN_SIZES)[:-1].tolist(), axis=-1)
    u = jax.nn.gelu(u)
    v = layer_norm(jax.nn.gelu(v), ln_v_g, ln_v_b)
    a = chunk_spatial_gate(u, v, w_spatial, b_spatial)
    q = rope(q.reshape(B, S, N_HEADS, HEAD_DIM), pos)
    k = rope(k.reshape(B, S, N_KV_HEADS, HEAD_DIM), pos)
    vv = vv.reshape(B, S, N_KV_HEADS, HEAD_DIM)
    qi = rope(qi.reshape(B, S, IDX_HEADS, IDX_DIM), pos)
    ki = rope(ki.reshape(B, S, 1, IDX_DIM), pos)[:, :, 0]
    wi = wi * (IDX_HEADS ** -0.5)
    o = attend(q, k, vv, qi, ki, wi)
    merged = jax.nn.sigmoid(ga) * (a @ w_branch_a) + jax.nn.sigmoid(gb) * (o @ w_branch_b)
    return merged @ w_out, k, vv, ki, v


def cross_attend(h, mem_k, mem_v, w_mem_q, w_mem_o):
    B, S, _ = h.shape
    q = (h @ w_mem_q).reshape(B, S, MEM_HEADS, MEM_HEAD_DIM)
    s = jnp.einsum('bqhd,bmhd->bhqm', q, mem_k).astype(jnp.float32) * (MEM_HEAD_DIM ** -0.5)
    p = jax.nn.softmax(s, axis=-1).astype(mem_v.dtype)
    o = jnp.einsum('bhqm,bmhd->bqhd', p, mem_v)
    return o.reshape(B, S, MEM_HEADS * MEM_HEAD_DIM) @ w_mem_o


def conv_ffn(h, conv_state, w_up, conv_w, conv_b, w_down):
    S = h.shape[1]
    up = h @ w_up
    ext = jnp.concatenate([conv_state.astype(up.dtype), up], axis=1)
    c = conv_b
    for j in range(CONV_WIDTH):
        c = c + conv_w[j] * ext[:, j:j + S]
    gate, val = jnp.split(c, 2, axis=-1)
    y = (jax.nn.gelu(gate) * val) @ w_down
    return y, ext[:, -(CONV_WIDTH - 1):]


def decoder_layer(x, pos, attend, mem_k, mem_v, conv_state, norms, w_in, ln_v_g, ln_v_b, w_spatial, b_spatial,
                  w_branch_a, w_branch_b, w_out, w_mem_q, w_mem_o, w_up, conv_w, conv_b, w_down):
    h = rms_norm(x, norms[0])
    mix, k, vv, ki, vrows = parallel_mixer(h, pos, attend, w_in, ln_v_g, ln_v_b, w_spatial, b_spatial,
                                           w_branch_a, w_branch_b, w_out)
    x = x + rms_norm(mix, norms[1])
    h = rms_norm(x, norms[2])
    x = x + rms_norm(cross_attend(h, mem_k, mem_v, w_mem_q, w_mem_o), norms[3])
    h = rms_norm(x, norms[4])
    f, new_conv = conv_ffn(h, conv_state, w_up, conv_w, conv_b, w_down)
    x = x + rms_norm(f, norms[5])
    return x, k, vv, ki, vrows, new_conv


def setup_inputs(seed: int = 0) -> dict:
    key = jax.random.key(seed)
    ks = jax.random.split(key, 32)
    f32 = jnp.float32
    n_pages = PAST_LEN // PAGE_SIZE
    n_pool = (DEC_BATCH * n_pages * 5) // 4
    nrm = lambda k, shape, scale: jax.random.normal(k, shape, f32) * scale
    page_table = jax.random.permutation(ks[0], n_pool)[:DEC_BATCH * n_pages].reshape(DEC_BATCH, n_pages).astype(jnp.int32)
    return {
        "x_prompt": nrm(ks[1], (BATCH, SEQ, D_MODEL), 1.0),
        "x_sample": nrm(ks[2], (DEC_BATCH, DEC_SEQ, D_MODEL), 1.0),
        "cache_k": nrm(ks[3], (DEPTH, n_pool, PAGE_SIZE, N_KV_HEADS, HEAD_DIM), 1.0),
        "cache_v": nrm(ks[4], (DEPTH, n_pool, PAGE_SIZE, N_KV_HEADS, HEAD_DIM), 1.0),
        "cache_idx_k": nrm(ks[5], (DEPTH, n_pool, PAGE_SIZE, IDX_DIM), 1.0),
        "cache_mem_k": nrm(ks[6], (DEPTH, DEC_BATCH, MEM_TOKENS, MEM_HEADS, MEM_HEAD_DIM), 1.0),
        "cache_mem_v": nrm(ks[7], (DEPTH, DEC_BATCH, MEM_TOKENS, MEM_HEADS, MEM_HEAD_DIM), 1.0),
        "state_conv": nrm(ks[8], (DEPTH, DEC_BATCH, CONV_WIDTH - 1, 2 * D_FF), 1.0),
        "page_table": page_table,
        "mem_prompt": nrm(ks[9], (BATCH, MEM_TOKENS, D_MODEL), 1.0),
        "norms": 1.0 + nrm(ks[10], (DEPTH, N_NORMS, D_MODEL), 0.01),
        "w_in": nrm(ks[11], (DEPTH, D_MODEL, IN_WIDTH), D_MODEL ** -0.5),
        "ln_v_g": 1.0 + nrm(ks[12], (DEPTH, A_WIDTH), 0.01),
        "ln_v_b": nrm(ks[13], (DEPTH, A_WIDTH), 0.01),
        "w_spatial": nrm(ks[14], (DEPTH, A_GROUPS, CHUNK, CHUNK), CHUNK ** -0.5),
        "b_spatial": 1.0 + nrm(ks[15], (DEPTH, A_GROUPS, CHUNK), 0.01),
        "w_branch_a": nrm(ks[16], (DEPTH, A_WIDTH, D_MODEL), A_WIDTH ** -0.5),
        "w_branch_b": nrm(ks[17], (DEPTH, N_HEADS * HEAD_DIM, D_MODEL), (N_HEADS * HEAD_DIM) ** -0.5),
        "w_out": nrm(ks[18], (DEPTH, D_MODEL, D_MODEL), D_MODEL ** -0.5),
        "w_mem_q": nrm(ks[19], (DEPTH, D_MODEL, MEM_HEADS * MEM_HEAD_DIM), D_MODEL ** -0.5),
        "w_mem_kv": nrm(ks[20], (DEPTH, D_MODEL, 2 * MEM_HEADS * MEM_HEAD_DIM), D_MODEL ** -0.5),
        "w_mem_o": nrm(ks[21], (DEPTH, MEM_HEADS * MEM_HEAD_DIM, D_MODEL), (MEM_HEADS * MEM_HEAD_DIM) ** -0.5),
        "w_up": nrm(ks[22], (DEPTH, D_MODEL, 2 * D_FF), D_MODEL ** -0.5),
        "conv_w": nrm(ks[23], (DEPTH, CONV_WIDTH, 2 * D_FF), CONV_WIDTH ** -0.5),
        "conv_b": nrm(ks[24], (DEPTH, 2 * D_FF), 0.01),
        "w_down": nrm(ks[25], (DEPTH, D_FF, D_MODEL), D_FF ** -0.5),
    }


def reference(x_prompt, x_sample, cache_k, cache_v, cache_idx_k, cache_mem_k, cache_mem_v, state_conv, page_table,
              mem_prompt, norms, w_in, ln_v_g, ln_v_b, w_spatial, b_spatial, w_branch_a, w_branch_b, w_out,
              w_mem_q, w_mem_kv, w_mem_o, w_up, conv_w, conv_b, w_down):
    B, S, _ = x_prompt.shape
    Bd, T, _ = x_sample.shape
    M = mem_prompt.shape[1]
    past = page_table.shape[1] * PAGE_SIZE
    pos_p = jnp.arange(S)
    pos_s = past + jnp.arange(T)
    last_chunk_start = ((S - 1) // CHUNK) * CHUNK
    yp, ys = x_prompt, x_sample
    kp_l, vp_l, kip_l, ks_l, vs_l, kis_l = [], [], [], [], [], []
    mk_l, mv_l, vrp_l, vrs_l, cp_l, cs_l = [], [], [], [], [], []
    for l in range(DEPTH):
        lw = (norms[l], w_in[l], ln_v_g[l], ln_v_b[l], w_spatial[l], b_spatial[l], w_branch_a[l], w_branch_b[l],
              w_out[l], w_mem_q[l], w_mem_o[l], w_up[l], conv_w[l], conv_b[l], w_down[l])
        mk, mv = jnp.split(mem_prompt @ w_mem_kv[l], 2, axis=-1)
        mk = mk.reshape(B, M, MEM_HEADS, MEM_HEAD_DIM)
        mv = mv.reshape(B, M, MEM_HEADS, MEM_HEAD_DIM)
        conv0 = jnp.zeros((B, CONV_WIDTH - 1, 2 * D_FF), yp.dtype)
        yp, kp, vp, kip, vrp, cp = decoder_layer(yp, pos_p, dsa_prompt, mk, mv, conv0, *lw)
        attend_s = functools.partial(dsa_sample, cache_k=cache_k, cache_v=cache_v, cache_idx_k=cache_idx_k,
                                     page_table=page_table, layer=l)
        ys, ks_, vs_, kis, vrs, cs = decoder_layer(ys, pos_s, attend_s, cache_mem_k[l], cache_mem_v[l],
                                                   state_conv[l], *lw)
        kp_l.append(kp); vp_l.append(vp); kip_l.append(kip)
        ks_l.append(ks_); vs_l.append(vs_); kis_l.append(kis)
        mk_l.append(mk); mv_l.append(mv)
        vrp_l.append(vrp[:, last_chunk_start:]); vrs_l.append(vrs)
        cp_l.append(cp); cs_l.append(cs)
    return (yp, ys,
            jnp.stack(kp_l), jnp.stack(vp_l), jnp.stack(kip_l),
            jnp.stack(ks_l), jnp.stack(vs_l), jnp.stack(kis_l),
            jnp.stack(mk_l), jnp.stack(mv_l),
            jnp.stack(vrp_l), jnp.stack(vrs_l),
            jnp.stack(cp_l), jnp.stack(cs_l))
```

```python
import functools

import numpy as np
import jax
import jax.numpy as jnp
from jax import lax
from jax.experimental import pallas as pl
from jax.experimental.pallas import tpu as pltpu

F32 = jnp.float32
BF16 = jnp.bfloat16
I32 = jnp.int32

D_MODEL = 1024
PAGE = 128
CHUNK = 128
A_GROUPS = 4
A_GROUP_DIM = 128
A_WIDTH = A_GROUPS * A_GROUP_DIM
N_HEADS = 8
N_KV = 4
HEAD_DIM = 128
GQA = N_HEADS // N_KV
KV_WIDTH = N_KV * HEAD_DIM
IDX_HEADS = 8
IDX_DIM = 64
TOPK_MAX = 256
ROPE_THETA = 10000.0
MEM_HEADS = 4
MEM_HEAD_DIM = 128
MEM_WIDTH = MEM_HEADS * MEM_HEAD_DIM
D_FF = 2816
CONV_WIDTH = 3
RMS_EPS = 1e-6
LN_EPS = 1e-5
IN_SIZES = (A_WIDTH, A_WIDTH, N_HEADS * HEAD_DIM, KV_WIDTH, KV_WIDTH,
            IDX_HEADS * IDX_DIM, IDX_DIM, IDX_HEADS, D_MODEL, D_MODEL)

LANES = 128
ROW_TILE = 512
KEY_BLOCK = 512
FF_CHUNK = 256
VMEM_LIMIT = 56 * 1024 * 1024
INT_MIN = -2147483648
NEG_BIAS = -1e30


def _cparams(sem):
    return pltpu.CompilerParams(dimension_semantics=sem, vmem_limit_bytes=VMEM_LIMIT)


def _const_spec(shape):
    nd = len(shape)
    return pl.BlockSpec(shape, lambda *_: (0,) * nd, pipeline_mode=pl.Buffered(1))


def _gelu(x):
    return 0.5 * x * (1.0 + jnp.tanh(0.7978845608028654 * (x + 0.044715 * (x * x * x))))


def _rms(x, g):
    return x * lax.rsqrt(jnp.mean(x * x, axis=-1, keepdims=True) + RMS_EPS) * g


def _dot(a, b):
    return jnp.dot(a, b, preferred_element_type=F32)


def _dot_t(a, b):
    return lax.dot_general(a, b, (((1,), (1,)), ((), ())), preferred_element_type=F32)


def _float_key(x):
    b = pltpu.bitcast(x, I32)
    return jnp.where(b < 0, b ^ jnp.int32(0x7FFFFFFF), b)


def _memkv_kernel(x_ref, w_ref, kf_ref, vf_ref, kb_ref, vb_ref):
    z = _dot(x_ref[...].astype(BF16), w_ref[...])
    k = z[:, :MEM_WIDTH]
    v = z[:, MEM_WIDTH:]
    kf_ref[...] = k
    vf_ref[...] = v
    kb_ref[...] = k.astype(BF16)
    vb_ref[...] = v.astype(BF16)


def _memkv(mem2d, w_bf):
    m = mem2d.shape[0]
    tm = min(ROW_TILE, m)
    out = [jax.ShapeDtypeStruct((m, MEM_WIDTH), F32)] * 2 + [jax.ShapeDtypeStruct((m, MEM_WIDTH), BF16)] * 2
    row = lambda w: pl.BlockSpec((tm, w), lambda i: (i, 0))
    return pl.pallas_call(
        _memkv_kernel,
        grid=(m // tm,),
        in_specs=[row(D_MODEL), _const_spec((D_MODEL, 2 * MEM_WIDTH))],
        out_specs=[row(MEM_WIDTH)] * 4,
        out_shape=out,
        compiler_params=_cparams(("parallel",)),
        name="memkv",
    )(mem2d, w_bf)


def _rope_full(z, cos, sin_signed):
    return z * cos + pltpu.roll(z, HEAD_DIM // 2, axis=1) * sin_signed


def _rope_idx(z, cos, sin_signed, lane):
    partner = jnp.where((lane % IDX_DIM) < IDX_DIM // 2,
                        pltpu.roll(z, LANES - IDX_DIM // 2, axis=1),
                        pltpu.roll(z, IDX_DIM // 2, axis=1))
    return z * cos + partner * sin_signed


def _in_proj_kernel(chunk, x_ref, g_ref, wu, wv, wq, wk, wvv, wqi, wkw, wga, wgb, lng, lnb, wsp, bsp,
                    c128, s128, c64, s64,
                    a_ref, vrow_ref, q_ref, kf_ref, kb_ref, vf_ref, vb_ref, qi_ref, kw_ref, kwb_ref,
                    sga_ref, sgb_ref):
    tm = x_ref.shape[0]
    h = _rms(x_ref[...], g_ref[...]).astype(BF16)
    lane = lax.broadcasted_iota(I32, (tm, LANES), 1)

    gu = _gelu(_dot(h, wu[...]))
    gv = _gelu(_dot(h, wv[...]))
    mu = jnp.mean(gv, axis=-1, keepdims=True)
    vc = gv - mu
    var = jnp.mean(vc * vc, axis=-1, keepdims=True)
    vln = vc * lax.rsqrt(var + LN_EPS) * lng[...] + lnb[...]
    if chunk == 1:
        vrow_ref[...] = vln
        a_ref[...] = (gu * (vln * wsp[...] + bsp[...])).astype(BF16)
    else:
        vrow_ref[0] = vln[tm - chunk:, :]
        tri = (lax.broadcasted_iota(I32, (chunk, chunk), 0) >= lax.broadcasted_iota(I32, (chunk, chunk), 1))
        vlb = vln.astype(BF16)
        for g in range(A_GROUPS):
            wt = jnp.where(tri, wsp[g], 0.0).astype(BF16)
            cs = slice(g * A_GROUP_DIM, (g + 1) * A_GROUP_DIM)
            for c in range(tm // chunk):
                rs = slice(c * chunk, (c + 1) * chunk)
                sv = _dot(wt, vlb[rs, cs]) + bsp[:, cs]
                a_ref[rs, cs] = (gu[rs, cs] * sv).astype(BF16)

    cos = c128[...]
    sin = s128[...]
    zq = _dot(h, wq[...])
    for hh in range(N_HEADS):
        cs = slice(hh * HEAD_DIM, (hh + 1) * HEAD_DIM)
        q_ref[:, cs] = (_rope_full(zq[:, cs], cos, sin) * (HEAD_DIM ** -0.5)).astype(BF16)
    zk = _dot(h, wk[...])
    for hh in range(N_KV):
        cs = slice(hh * HEAD_DIM, (hh + 1) * HEAD_DIM)
        kr = _rope_full(zk[:, cs], cos, sin)
        kf_ref[:, cs] = kr
        kb_ref[:, cs] = kr.astype(BF16)
    zv = _dot(h, wvv[...])
    vf_ref[...] = zv
    vb_ref[...] = zv.astype(BF16)

    ci = c64[...]
    si = s64[...]
    zqi = _dot(h, wqi[...])
    low = lane < IDX_DIM
    for p in range(IDX_HEADS // 2):
        r = _rope_idx(zqi[:, p * LANES:(p + 1) * LANES], ci, si, lane)
        qi_ref[:, (2 * p) * LANES:(2 * p + 1) * LANES] = jnp.where(low, r, 0.0).astype(BF16)
        qi_ref[:, (2 * p + 1) * LANES:(2 * p + 2) * LANES] = jnp.where(
            low, pltpu.roll(r, IDX_DIM, axis=1), 0.0).astype(BF16)
    zkw = _dot(h, wkw[...])
    kir = _rope_idx(zkw, ci, si, lane)
    wscale = (IDX_HEADS ** -0.5) * (IDX_DIM ** -0.5)
    kw_ref[...] = jnp.where(low, kir, zkw * wscale)
    kwb_ref[...] = jnp.where(low, kir, 0.0).astype(BF16)

    sga_ref[...] = jax.nn.sigmoid(_dot(h, wga[...])).astype(BF16)
    sgb_ref[...] = jax.nn.sigmoid(_dot(h, wgb[...])).astype(BF16)


def _in_proj(x2d, lw, tabs, seq, chunk):
    m = x2d.shape[0]
    tm = min(ROW_TILE, seq) if chunk != 1 else m
    nt = seq // tm if chunk != 1 else 1
    nb = m // seq if chunk != 1 else 1
    row = lambda w: pl.BlockSpec((tm, w), lambda i: (i, 0))
    tab = pl.BlockSpec((tm, LANES), lambda i: (i % nt, 0))
    if chunk == 1:
        vrow_spec = row(A_WIDTH)
        vrow_shape = jax.ShapeDtypeStruct((m, A_WIDTH), F32)
        sp_specs = [_const_spec((1, A_WIDTH)), _const_spec((1, A_WIDTH))]
    else:
        vrow_spec = pl.BlockSpec((1, chunk, A_WIDTH), lambda i: (i // nt, 0, 0))
        vrow_shape = jax.ShapeDtypeStruct((nb, chunk, A_WIDTH), F32)
        sp_specs = [_const_spec((A_GROUPS, chunk, chunk)), _const_spec((chunk, A_WIDTH))]
    widths = [(A_WIDTH, BF16), None, (N_HEADS * HEAD_DIM, BF16), (KV_WIDTH, F32), (KV_WIDTH, BF16),
              (KV_WIDTH, F32), (KV_WIDTH, BF16), (IDX_HEADS * LANES, BF16), (LANES, F32), (LANES, BF16),
              (D_MODEL, BF16), (D_MODEL, BF16)]
    out_specs = [vrow_spec if w is None else row(w[0]) for w in widths]
    out_shape = [vrow_shape if w is None else jax.ShapeDtypeStruct((m, w[0]), w[1]) for w in widths]
    wshapes = [(D_MODEL, A_WIDTH), (D_MODEL, A_WIDTH), (D_MODEL, N_HEADS * HEAD_DIM), (D_MODEL, KV_WIDTH),
               (D_MODEL, KV_WIDTH), (D_MODEL, IDX_HEADS * IDX_DIM), (D_MODEL, LANES), (D_MODEL, D_MODEL),
               (D_MODEL, D_MODEL)]
    in_specs = ([row(D_MODEL), _const_spec((1, D_MODEL))] + [_const_spec(s) for s in wshapes]
                + [_const_spec((1, A_WIDTH)), _const_spec((1, A_WIDTH))] + sp_specs + [tab] * 4)
    sp = (lw["sp_w1"], lw["sp_b1"]) if chunk == 1 else (lw["sp_w"], lw["sp_b"])
    return pl.pallas_call(
        functools.partial(_in_proj_kernel, chunk),
        grid=(m // tm,),
        in_specs=in_specs,
        out_specs=out_specs,
        out_shape=out_shape,
        compiler_params=_cparams(("arbitrary",)),
        name="in_proj",
    )(x2d, lw["g0"], lw["wu"], lw["wv"], lw["wq"], lw["wk"], lw["wvv"], lw["wqi"], lw["wkw"], lw["wga"],
      lw["wgb"], lw["ln_g"], lw["ln_b"], sp[0], sp[1], *tabs)


def _topk_threshold(count_ge, topk, shape):
    t0 = jnp.where(count_ge(jnp.zeros(shape, I32)) >= topk, jnp.int32(0), jnp.int32(INT_MIN))

    def bit_body(b, t):
        cand = t + lax.shift_left(jnp.int32(1), jnp.int32(30) - b)
        return jnp.where(count_ge(cand) >= topk, cand, t)

    return lax.fori_loop(0, 31, bit_body, jnp.broadcast_to(t0, shape))


def _dsa_prompt_kernel(topk, kb, q_ref, qi_ref, kw_ref, kib_ref, k_ref, v_ref, o_ref, qis, wib, keys, bias):
    tq = q_ref.shape[1]
    seq = k_ref.shape[1]
    i = pl.program_id(1)
    nkb = ((i + 1) * tq + kb - 1) // kb
    ncol = kb // LANES
    lane = lax.broadcasted_iota(I32, (tq, LANES), 1)
    rowpos = i * tq + lax.broadcasted_iota(I32, (tq, LANES), 0)
    shape = (tq, LANES)

    for h in range(IDX_HEADS):
        qis[h * tq:(h + 1) * tq, :] = qi_ref[0, :, h * LANES:(h + 1) * LANES]
        wib[h] = jnp.broadcast_to(kw_ref[0, :, IDX_DIM + h:IDX_DIM + h + 1], shape)

    def score_body(j, carry):
        base = pl.multiple_of(j * kb, kb)
        lg = _dot_t(qis[...], kib_ref[0, pl.ds(base, kb), :])
        for c in range(ncol):
            acc = jnp.zeros(shape, F32)
            for h in range(IDX_HEADS):
                acc = acc + wib[h] * jnp.maximum(lg[h * tq:(h + 1) * tq, c * LANES:(c + 1) * LANES], 0.0)
            col = base + c * LANES + lane
            key = jnp.where(col <= rowpos, _float_key(acc), jnp.int32(INT_MIN))
            keys[:, pl.ds(pl.multiple_of(base + c * LANES, LANES), LANES)] = key
        return carry

    lax.fori_loop(0, nkb, score_body, 0)

    def count(pred):
        def body(j, part):
            base = pl.multiple_of(j * kb, kb)
            blk = keys[:, pl.ds(base, kb)]
            for c in range(ncol):
                col = base + c * LANES + lane
                part = part + jnp.where(pred(blk[:, c * LANES:(c + 1) * LANES], col), 1.0, 0.0)
            return part
        part = lax.fori_loop(0, nkb, body, jnp.zeros(shape, F32))
        return jnp.broadcast_to(jnp.sum(part, axis=1, keepdims=True), shape)

    thr = _topk_threshold(lambda cand: count(lambda k, col: k >= cand), float(topk), shape)
    thr = jnp.maximum(thr, jnp.int32(INT_MIN + 1))

    def write_bias(sel):
        def body(j, carry):
            base = pl.multiple_of(j * kb, kb)
            blk = keys[:, pl.ds(base, kb)]
            for c in range(ncol):
                col = base + c * LANES + lane
                bias[:, pl.ds(pl.multiple_of(base + c * LANES, LANES), LANES)] = jnp.where(
                    sel(blk[:, c * LANES:(c + 1) * LANES], col), 0.0, NEG_BIAS)
            return carry
        lax.fori_loop(0, nkb, body, 0)

    n_ge = count(lambda k, col: k >= thr)
    has_ties = jnp.max(n_ge) > float(topk)

    @pl.when(jnp.logical_not(has_ties))
    def _():
        write_bias(lambda k, col: k >= thr)

    @pl.when(has_ties)
    def _():
        need = float(topk) - count(lambda k, col: k > thr)
        nbits = int(np.ceil(np.log2(seq))) + 1

        def bit_body(b, p):
            cand = p + lax.shift_left(jnp.int32(1), jnp.int32(nbits - 1) - b)
            below = count(lambda k, col: jnp.logical_and(k == thr, col < cand))
            return jnp.where(below < need, cand, p)

        last = lax.fori_loop(0, nbits, bit_body, jnp.zeros(shape, I32))
        write_bias(lambda k, col: jnp.logical_or(k > thr, jnp.logical_and(k == thr, col <= last)))

    for n in range(N_KV):
        qn = jnp.concatenate([q_ref[0, :, (GQA * n + g) * HEAD_DIM:(GQA * n + g + 1) * HEAD_DIM]
                              for g in range(GQA)], axis=0)
        cs = slice(n * HEAD_DIM, (n + 1) * HEAD_DIM)

        def att_body(j, carry):
            m, l, acc = carry
            base = pl.multiple_of(j * kb, kb)
            s = _dot_t(qn, k_ref[0, pl.ds(base, kb), cs])
            bb = bias[:, pl.ds(base, kb)]
            s = s + jnp.concatenate([bb] * GQA, axis=0)
            m_new = jnp.maximum(m, jnp.max(s, axis=1, keepdims=True))
            p = jnp.exp(s - m_new)
            alpha = jnp.exp(m - m_new)
            l = alpha * l + jnp.sum(p, axis=1, keepdims=True)
            acc = alpha * acc + _dot(p.astype(BF16), v_ref[0, pl.ds(base, kb), cs])
            return m_new, l, acc

        m0 = jnp.full((GQA * tq, 1), -jnp.inf, F32)
        l0 = jnp.zeros((GQA * tq, 1), F32)
        a0 = jnp.zeros((GQA * tq, HEAD_DIM), F32)
        _, l, acc = lax.fori_loop(0, nkb, att_body, (m0, l0, a0))
        o = acc / l
        for g in range(GQA):
            hh = GQA * n + g
            o_ref[0, :, hh * HEAD_DIM:(hh + 1) * HEAD_DIM] = o[g * tq:(g + 1) * tq].astype(BF16)


def _dsa_prompt(q, qi, kw, kib, kb16, vb16, nb, seq):
    tq = min(128, seq)
    kb = min(KEY_BLOCK, seq)
    topk = min(TOPK_MAX, seq // 4)
    r3 = lambda a: a.reshape(nb, seq, a.shape[-1])
    qblk = lambda w: pl.BlockSpec((1, tq, w), lambda b, i: (b, i, 0))
    full = lambda w: pl.BlockSpec((1, seq, w), lambda b, i: (b, 0, 0))
    out = pl.pallas_call(
        functools.partial(_dsa_prompt_kernel, topk, kb),
        grid=(nb, seq // tq),
        in_specs=[qblk(N_HEADS * HEAD_DIM), qblk(IDX_HEADS * LANES), qblk(LANES), full(LANES), full(KV_WIDTH),
                  full(KV_WIDTH)],
        out_specs=qblk(N_HEADS * HEAD_DIM),
        out_shape=jax.ShapeDtypeStruct((nb, seq, N_HEADS * HEAD_DIM), BF16),
        scratch_shapes=[pltpu.VMEM((IDX_HEADS * tq, LANES), BF16), pltpu.VMEM((IDX_HEADS, tq, LANES), F32),
                        pltpu.VMEM((tq, seq), I32), pltpu.VMEM((tq, seq), F32)],
        compiler_params=_cparams(("parallel", "arbitrary")),
        name="dsa_prompt",
    )(r3(q), r3(qi), r3(kw), r3(kib), r3(kb16), r3(vb16))
    return out.reshape(nb * seq, N_HEADS * HEAD_DIM)


def _dsa_sample_kernel(topk, npages, pt_ref, q_ref, qi_ref, kw_ref, kwb_ref, kn_ref, vn_ref, cik_ref, ck_ref, cv_ref,
                       o_ref, ibuf, kbuf, vbuf, keys, bias, isem, ksem, vsem):
    b = pl.program_id(0)
    past = npages * PAGE
    lp = past + LANES
    lane = lax.broadcasted_iota(I32, (1, LANES), 1)

    def idx_copy(j):
        return pltpu.make_async_copy(cik_ref.at[pt_ref[b, j]], ibuf.at[j], isem)

    def kv_copies(j, slot):
        pg = pt_ref[b, j]
        return (pltpu.make_async_copy(ck_ref.at[pg], kbuf.at[slot], ksem.at[slot]),
                pltpu.make_async_copy(cv_ref.at[pg], vbuf.at[slot], vsem.at[slot]))

    def start_idx(j, c):
        idx_copy(j).start()
        return c

    lax.fori_loop(0, npages, start_idx, 0)
    for cp in kv_copies(0, 0):
        cp.start()

    qif = qi_ref[0].astype(F32)
    qi8f = jnp.concatenate([qif[:, h * LANES:h * LANES + IDX_DIM] for h in range(IDX_HEADS)], axis=0)
    qi8 = qi8f.astype(BF16)
    kwrow = kw_ref[0]
    eye = (lax.broadcasted_iota(I32, (IDX_HEADS, LANES), 0) + IDX_DIM
           == lax.broadcasted_iota(I32, (IDX_HEADS, LANES), 1))
    wcol = jnp.sum(jnp.where(eye, jnp.broadcast_to(kwrow, (IDX_HEADS, LANES)), 0.0), axis=1, keepdims=True)

    def wait_idx(j, c):
        idx_copy(j).wait()
        return c

    lax.fori_loop(0, npages, wait_idx, 0)

    def score_body(j, c):
        lg = _dot_t(qi8, ibuf[j].astype(BF16))
        sc = jnp.sum(wcol * jnp.maximum(lg, 0.0), axis=0, keepdims=True)
        keys[:, pl.ds(pl.multiple_of(j * PAGE, PAGE), PAGE)] = _float_key(sc + 0.0)
        return c

    lax.fori_loop(0, npages, score_body, 0)
    lgs = jnp.sum(qi8f * kwb_ref[0].astype(F32)[:, :IDX_DIM], axis=1, keepdims=True)
    scs = jnp.sum(wcol * jnp.maximum(lgs, 0.0), axis=0, keepdims=True)
    keys[:, past:lp] = jnp.where(lane == 0, _float_key(jnp.broadcast_to(scs, (1, LANES)) + 0.0),
                                 jnp.int32(INT_MIN))

    shape = (1, LANES)
    nblk = lp // LANES

    def count(pred):
        def body(j, part):
            base = pl.multiple_of(j * LANES, LANES)
            return part + jnp.where(pred(keys[:, pl.ds(base, LANES)], base + lane), 1.0, 0.0)
        part = lax.fori_loop(0, nblk, body, jnp.zeros(shape, F32))
        return jnp.broadcast_to(jnp.sum(part, axis=1, keepdims=True), shape)

    thr = _topk_threshold(lambda cand: count(lambda k, col: k >= cand), float(topk), shape)
    thr = jnp.maximum(thr, jnp.int32(INT_MIN + 1))

    def write_bias(sel):
        def body(j, c):
            base = pl.multiple_of(j * LANES, LANES)
            bias[:, pl.ds(base, LANES)] = jnp.where(sel(keys[:, pl.ds(base, LANES)], base + lane), 0.0, NEG_BIAS)
            return c
        lax.fori_loop(0, nblk, body, 0)

    n_ge = count(lambda k, col: k >= thr)
    has_ties = jnp.max(n_ge) > float(topk)

    @pl.when(jnp.logical_not(has_ties))
    def _():
        write_bias(lambda k, col: k >= thr)

    @pl.when(has_ties)
    def _():
        need = float(topk) - count(lambda k, col: k > thr)
        nbits = int(np.ceil(np.log2(lp))) + 1

        def bit_body(bb, p):
            cand = p + lax.shift_left(jnp.int32(1), jnp.int32(nbits - 1) - bb)
            below = count(lambda k, col: jnp.logical_and(k == thr, col < cand))
            return jnp.where(below < need, cand, p)

        last = lax.fori_loop(0, nbits, bit_body, jnp.zeros(shape, I32))
        write_bias(lambda k, col: jnp.logical_or(k > thr, jnp.logical_and(k == thr, col <= last)))

    q8 = q_ref[0].astype(F32)
    hrow = lax.broadcasted_iota(I32, (N_HEADS, KV_WIDTH), 0)
    hcol = lax.broadcasted_iota(I32, (N_HEADS, KV_WIDTH), 1)
    own = (hcol // HEAD_DIM) == (hrow // GQA)
    qh = jnp.concatenate([q8[:, h * HEAD_DIM:(h + 1) * HEAD_DIM] for h in range(N_HEADS)], axis=0)
    qblkf = jnp.where(own, jnp.concatenate([qh] * N_KV, axis=1), 0.0)
    qblk = qblkf.astype(BF16)

    def att_step(s, vblk, carry):
        m, l, acc = carry
        m_new = jnp.maximum(m, jnp.max(s, axis=1, keepdims=True))
        p = jnp.exp(s - m_new)
        alpha = jnp.exp(m - m_new)
        l = alpha * l + jnp.sum(p, axis=1, keepdims=True)
        acc = alpha * acc + _dot(p.astype(BF16), vblk)
        return m_new, l, acc

    def att_body(j, carry):
        slot = j % 2

        @pl.when(j + 1 < npages)
        def _():
            for cp in kv_copies(j + 1, 1 - slot):
                cp.start()

        for cp in kv_copies(j, slot):
            cp.wait()
        s = _dot_t(qblk, kbuf[slot].astype(BF16)) + bias[:, pl.ds(pl.multiple_of(j * PAGE, PAGE), PAGE)]
        return att_step(s, vbuf[slot].astype(BF16), carry)

    m0 = jnp.full((N_HEADS, 1), -jnp.inf, F32)
    l0 = jnp.zeros((N_HEADS, 1), F32)
    a0 = jnp.zeros((N_HEADS, KV_WIDTH), F32)
    m, l, acc = lax.fori_loop(0, npages, att_body, (m0, l0, a0))
    kself = kn_ref[0].astype(BF16).astype(F32)
    vself = vn_ref[0].astype(BF16).astype(F32)
    s = jnp.sum(qblkf * kself, axis=1, keepdims=True) + bias[:, past:past + 1]
    m_new = jnp.maximum(m, s)
    p = jnp.exp(s - m_new)
    alpha = jnp.exp(m - m_new)
    l = alpha * l + p
    acc = alpha * acc + p.astype(BF16).astype(F32) * vself
    o = jnp.where(own, acc / l, 0.0)
    osum = o[:, 0:HEAD_DIM]
    for n in range(1, N_KV):
        osum = osum + o[:, n * HEAD_DIM:(n + 1) * HEAD_DIM]
    for h in range(N_HEADS):
        o_ref[0, :, h * HEAD_DIM:(h + 1) * HEAD_DIM] = osum[h:h + 1, :].astype(BF16)


def _dsa_sample(q, qi, kw, kwb, kf, vf, cik, ck, cv, page_table):
    nb, npages = page_table.shape
    topk = min(TOPK_MAX, (npages * PAGE + 1) // 4)
    lp = npages * PAGE + LANES
    r3 = lambda a: a.reshape(nb, 1, a.shape[-1])
    blk = lambda w: pl.BlockSpec((1, 1, w), lambda b, pt: (b, 0, 0))
    anyspec = pl.BlockSpec(memory_space=pl.ANY)
    grid_spec = pltpu.PrefetchScalarGridSpec(
        num_scalar_prefetch=1,
        grid=(nb,),
        in_specs=[blk(N_HEADS * HEAD_DIM), blk(IDX_HEADS * LANES), blk(LANES), blk(LANES), blk(KV_WIDTH),
                  blk(KV_WIDTH), anyspec, anyspec, anyspec],
        out_specs=blk(N_HEADS * HEAD_DIM),
        scratch_shapes=[pltpu.VMEM((npages, PAGE, IDX_DIM), F32), pltpu.VMEM((2, PAGE, KV_WIDTH), F32),
                        pltpu.VMEM((2, PAGE, KV_WIDTH), F32), pltpu.VMEM((1, lp), I32), pltpu.VMEM((1, lp), F32),
                        pltpu.SemaphoreType.DMA(()), pltpu.SemaphoreType.DMA((2,)), pltpu.SemaphoreType.DMA((2,))],
    )
    out = pl.pallas_call(
        functools.partial(_dsa_sample_kernel, topk, npages),
        grid_spec=grid_spec,
        out_shape=jax.ShapeDtypeStruct((nb, 1, N_HEADS * HEAD_DIM), BF16),
        compiler_params=_cparams(("arbitrary",)),
        name="dsa_sample",
    )(page_table, r3(q), r3(qi), r3(kw), r3(kwb), r3(kf), r3(vf), cik, ck, cv)
    return out.reshape(nb, N_HEADS * HEAD_DIM)


def _mix_out_kernel(a_ref, o_ref, sga_ref, sgb_ref, x_ref, wba, wbb, wout, g_ref, y_ref):
    merged = (sga_ref[...].astype(F32) * _dot(a_ref[...], wba[...])
              + sgb_ref[...].astype(F32) * _dot(o_ref[...], wbb[...]))
    mix = _dot(merged.astype(BF16), wout[...])
    y_ref[...] = x_ref[...] + _rms(mix, g_ref[...])


def _mix_out(a, o, sga, sgb, x2d, lw):
    m = x2d.shape[0]
    tm = min(ROW_TILE, m)
    row = lambda w: pl.BlockSpec((tm, w), lambda i: (i, 0))
    return pl.pallas_call(
        _mix_out_kernel,
        grid=(m // tm,),
        in_specs=[row(A_WIDTH), row(D_MODEL), row(D_MODEL), row(D_MODEL), row(D_MODEL),
                  _const_spec((A_WIDTH, D_MODEL)), _const_spec((D_MODEL, D_MODEL)), _const_spec((D_MODEL, D_MODEL)),
                  _const_spec((1, D_MODEL))],
        out_specs=row(D_MODEL),
        out_shape=jax.ShapeDtypeStruct((m, D_MODEL), F32),
        compiler_params=_cparams(("parallel",)),
        name="mix_out",
    )(a, o, sga, sgb, x2d, lw["wba"], lw["wbb"], lw["wout"], lw["g1"])


def _cross_prompt_kernel(x_ref, g2, g3, wq, wo, mk_ref, mv_ref, y_ref):
    x = x_ref[0]
    h = _rms(x, g2[...]).astype(BF16)
    qm = (_dot(h, wq[...]) * (MEM_HEAD_DIM ** -0.5)).astype(BF16)
    outs = []
    for hh in range(MEM_HEADS):
        cs = slice(hh * MEM_HEAD_DIM, (hh + 1) * MEM_HEAD_DIM)
        s = _dot_t(qm[:, cs], mk_ref[0, :, cs])
        p = jnp.exp(s - jnp.max(s, axis=1, keepdims=True))
        p = p / jnp.sum(p, axis=1, keepdims=True)
        outs.append(_dot(p.astype(BF16), mv_ref[0, :, cs]).astype(BF16))
    c = _dot(jnp.concatenate(outs, axis=1), wo[...])
    y_ref[0] = x + _rms(c, g3[...])


def _cross_prompt(x2d, mk, mv, lw, nb, seq):
    tm = min(ROW_TILE, seq)
    mt = mk.shape[0] // nb
    xb = pl.BlockSpec((1, tm, D_MODEL), lambda b, i: (b, i, 0))
    mb = pl.BlockSpec((1, mt, MEM_WIDTH), lambda b, i: (b, 0, 0))
    out = pl.pallas_call(
        _cross_prompt_kernel,
        grid=(nb, seq // tm),
        in_specs=[xb, _const_spec((1, D_MODEL)), _const_spec((1, D_MODEL)), _const_spec((D_MODEL, MEM_WIDTH)),
                  _const_spec((MEM_WIDTH, D_MODEL)), mb, mb],
        out_specs=xb,
        out_shape=jax.ShapeDtypeStruct((nb, seq, D_MODEL), F32),
        compiler_params=_cparams(("parallel", "parallel")),
        name="cross_prompt",
    )(x2d.reshape(nb, seq, D_MODEL), lw["g2"], lw["g3"], lw["wmq"], lw["wmo"],
      mk.reshape(nb, mt, MEM_WIDTH), mv.reshape(nb, mt, MEM_WIDTH))
    return out.reshape(nb * seq, D_MODEL)


def _cross_sample_kernel(x_ref, g2, g3, wq, wo, mk_ref, mv_ref, y_ref):
    x = x_ref[...]
    bt = x.shape[0]
    h = _rms(x, g2[...]).astype(BF16)
    qm = (_dot(h, wq[...]) * (MEM_HEAD_DIM ** -0.5)).astype(BF16)
    rows = []
    for b in range(bt):
        mkb = mk_ref[b].astype(BF16)
        mvb = mv_ref[b].astype(BF16)
        qb = jnp.broadcast_to(qm[b:b + 1, :], (8, MEM_WIDTH))
        outs = []
        for hh in range(MEM_HEADS):
            cs = slice(hh * MEM_HEAD_DIM, (hh + 1) * MEM_HEAD_DIM)
            s = _dot_t(qb[:, cs], mkb[:, cs])
            p = jnp.exp(s - jnp.max(s, axis=1, keepdims=True))
            p = p / jnp.sum(p, axis=1, keepdims=True)
            outs.append(_dot(p.astype(BF16), mvb[:, cs]))
        rows.append(jnp.concatenate(outs, axis=1)[0:1, :])
    att = jnp.concatenate(rows, axis=0).astype(BF16)
    c = _dot(att, wo[...])
    y_ref[...] = x + _rms(c, g3[...])


def _cross_sample(x2d, mk, mv, lw):
    nb = x2d.shape[0]
    bt = 8
    mt = mk.shape[1]
    xb = pl.BlockSpec((bt, D_MODEL), lambda i: (i, 0))
    mb = pl.BlockSpec((bt, mt, MEM_WIDTH), lambda i: (i, 0, 0))
    return pl.pallas_call(
        _cross_sample_kernel,
        grid=(nb // bt,),
        in_specs=[xb, _const_spec((1, D_MODEL)), _const_spec((1, D_MODEL)), _const_spec((D_MODEL, MEM_WIDTH)),
                  _const_spec((MEM_WIDTH, D_MODEL)), mb, mb],
        out_specs=xb,
        out_shape=jax.ShapeDtypeStruct((nb, D_MODEL), F32),
        compiler_params=_cparams(("parallel",)),
        name="cross_sample",
    )(x2d, lw["g2"], lw["g3"], lw["wmq"], lw["wmo"], mk.reshape(nb, mt, MEM_WIDTH), mv.reshape(nb, mt, MEM_WIDTH))


def _ffn_seq_kernel(x_ref, g4, g5, wup, cw, cb, wdn, st_ref, y_ref, ns_ref, carry, ext, acc):
    tm = x_ref.shape[1]
    t = pl.program_id(1)

    @pl.when(t == 0)
    def _():
        carry[6:8, :] = st_ref[0]

    x = x_ref[0]
    h = _rms(x, g4[...]).astype(BF16)
    for c in range(D_FF // FF_CHUNK):
        prod = None
        for half in range(2):
            cs = slice(half * D_FF + c * FF_CHUNK, half * D_FF + (c + 1) * FF_CHUNK)
            up = _dot(h, wup[:, cs])
            ext[half, 0:8, :] = carry[:, cs]
            ext[half, 8:8 + tm, :] = up
            carry[:, cs] = up[tm - 8:, :]
            conv = (cb[:, cs] + cw[0:1, cs] * ext[half, 6:6 + tm, :] + cw[1:2, cs] * ext[half, 7:7 + tm, :]
                    + cw[2:3, cs] * up)
            prod = _gelu(conv) if half == 0 else prod * conv
        part = _dot(prod.astype(BF16), wdn[c * FF_CHUNK:(c + 1) * FF_CHUNK, :])
        if c == 0:
            acc[...] = part
        else:
            acc[...] += part
    y_ref[0] = x + _rms(acc[...], g5[...])
    ns_ref[0] = carry[6:8, :]


def _ffn_seq(x2d, state, lw, nb, seq):
    tm = min(ROW_TILE, seq)
    xb = pl.BlockSpec((1, tm, D_MODEL), lambda b, t: (b, t, 0))
    sb = pl.BlockSpec((1, CONV_WIDTH - 1, 2 * D_FF), lambda b, t: (b, 0, 0))
    y, ns = pl.pallas_call(
        _ffn_seq_kernel,
        grid=(nb, seq // tm),
        in_specs=[xb, _const_spec((1, D_MODEL)), _const_spec((1, D_MODEL)), _const_spec((D_MODEL, 2 * D_FF)),
                  _const_spec((CONV_WIDTH, 2 * D_FF)), _const_spec((1, 2 * D_FF)), _const_spec((D_FF, D_MODEL)), sb],
        out_specs=[xb, sb],
        out_shape=[jax.ShapeDtypeStruct((nb, seq, D_MODEL), F32),
                   jax.ShapeDtypeStruct((nb, CONV_WIDTH - 1, 2 * D_FF), F32)],
        scratch_shapes=[pltpu.VMEM((8, 2 * D_FF), F32), pltpu.VMEM((2, tm + 8, FF_CHUNK), F32),
                        pltpu.VMEM((tm, D_MODEL), F32)],
        compiler_params=_cparams(("arbitrary", "arbitrary")),
        name="ffn_seq",
    )(x2d.reshape(nb, seq, D_MODEL), lw["g4"], lw["g5"], lw["wup"], lw["cw"], lw["cb"], lw["wdn"], state)
    return y.reshape(nb * seq, D_MODEL), ns


def _ffn_step_kernel(x_ref, g4, g5, wup, cw, cb, wdn, s0_ref, s1_ref, y_ref, up_ref):
    x = x_ref[...]
    h = _rms(x, g4[...]).astype(BF16)
    acc = None
    for c in range(D_FF // FF_CHUNK):
        prod = None
        for half in range(2):
            cs = slice(half * D_FF + c * FF_CHUNK, half * D_FF + (c + 1) * FF_CHUNK)
            up = _dot(h, wup[:, cs])
            up_ref[:, cs] = up
            conv = cb[:, cs] + cw[0:1, cs] * s0_ref[:, cs] + cw[1:2, cs] * s1_ref[:, cs] + cw[2:3, cs] * up
            prod = _gelu(conv) if half == 0 else prod * conv
        part = _dot(prod.astype(BF16), wdn[c * FF_CHUNK:(c + 1) * FF_CHUNK, :])
        acc = part if acc is None else acc + part
    y_ref[...] = x + _rms(acc, g5[...])


def _ffn_step(x2d, state, lw):
    nb = x2d.shape[0]
    s0 = state[:, 0, :]
    s1 = state[:, 1, :]
    full = lambda w: pl.BlockSpec((nb, w), lambda i: (0, 0))
    y, up = pl.pallas_call(
        _ffn_step_kernel,
        grid=(1,),
        in_specs=[full(D_MODEL), _const_spec((1, D_MODEL)), _const_spec((1, D_MODEL)),
                  _const_spec((D_MODEL, 2 * D_FF)), _const_spec((CONV_WIDTH, 2 * D_FF)), _const_spec((1, 2 * D_FF)),
                  _const_spec((D_FF, D_MODEL)), full(2 * D_FF), full(2 * D_FF)],
        out_specs=[full(D_MODEL), full(2 * D_FF)],
        out_shape=[jax.ShapeDtypeStruct((nb, D_MODEL), F32), jax.ShapeDtypeStruct((nb, 2 * D_FF), F32)],
        compiler_params=_cparams(("arbitrary",)),
        name="ffn_step",
    )(x2d, lw["g4"], lw["g5"], lw["wup"], lw["cw"], lw["cb"], lw["wdn"], s0, s1)
    return y, jnp.stack([s1, up], axis=1)


def _rope_tables(pos):
    def tab(dim, reps):
        half = dim // 2
        inv = ROPE_THETA ** (-jnp.arange(half, dtype=F32) / half)
        ang = pos.astype(F32)[:, None] * inv[None, :]
        cos = jnp.concatenate([jnp.cos(ang)] * 2, axis=1)
        sin = jnp.concatenate([-jnp.sin(ang), jnp.sin(ang)], axis=1)
        return jnp.tile(cos, (1, reps)), jnp.tile(sin, (1, reps))
    c128, s128 = tab(HEAD_DIM, 1)
    c64, s64 = tab(IDX_DIM, LANES // IDX_DIM)
    return c128, s128, c64, s64


def _layer_weights(l, norms, w_in, ln_v_g, ln_v_b, w_spatial, b_spatial, w_branch_a, w_branch_b, w_out, w_mem_q,
                   w_mem_kv, w_mem_o, w_up, conv_w, conv_b, w_down):
    offs = np.cumsum((0,) + IN_SIZES)
    seg = lambda i: w_in[l][:, offs[i]:offs[i + 1]].astype(BF16)
    wkw = jnp.concatenate([w_in[l][:, offs[6]:offs[8]],
                           jnp.zeros((D_MODEL, LANES - IDX_DIM - IDX_HEADS), F32)], axis=1).astype(BF16)
    lw = dict(
        wu=seg(0), wv=seg(1), wq=seg(2), wk=seg(3), wvv=seg(4), wqi=seg(5), wkw=wkw, wga=seg(8), wgb=seg(9),
        ln_g=ln_v_g[l][None, :], ln_b=ln_v_b[l][None, :],
        sp_w=w_spatial[l], sp_b=jnp.repeat(b_spatial[l].T, A_GROUP_DIM, axis=1),
        sp_w1=jnp.repeat(w_spatial[l][:, 0, 0], A_GROUP_DIM)[None, :],
        sp_b1=jnp.repeat(b_spatial[l][:, 0], A_GROUP_DIM)[None, :],
        wba=w_branch_a[l].astype(BF16), wbb=w_branch_b[l].astype(BF16), wout=w_out[l].astype(BF16),
        wmq=w_mem_q[l].astype(BF16), wmkv=w_mem_kv[l].astype(BF16), wmo=w_mem_o[l].astype(BF16),
        wup=w_up[l].astype(BF16), cw=conv_w[l], cb=conv_b[l][None, :], wdn=w_down[l].astype(BF16),
    )
    for i in range(6):
        lw["g%d" % i] = norms[l, i][None, :]
    return lw


def kernel(x_prompt, x_sample, cache_k, cache_v, cache_idx_k, cache_mem_k, cache_mem_v, state_conv, page_table,
           mem_prompt, norms, w_in, ln_v_g, ln_v_b, w_spatial, b_spatial, w_branch_a, w_branch_b, w_out,
           w_mem_q, w_mem_kv, w_mem_o, w_up, conv_w, conv_b, w_down):
    nb, seq, _ = x_prompt.shape
    nd, tdec, _ = x_sample.shape
    assert tdec == 1 and seq % CHUNK == 0
    depth = norms.shape[0]
    mt = mem_prompt.shape[1]
    npool = cache_k.shape[1]
    past = page_table.shape[1] * PAGE

    tabs_p = _rope_tables(jnp.arange(seq))
    tabs_s = tuple(jnp.broadcast_to(t, (nd, LANES)) for t in _rope_tables(jnp.full((1,), past)))
    yp = x_prompt.reshape(nb * seq, D_MODEL)
    ys = x_sample.reshape(nd, D_MODEL)
    mem2d = mem_prompt.reshape(nb * mt, D_MODEL)
    zero_state = jnp.zeros((nb, CONV_WIDTH - 1, 2 * D_FF), F32)
    ck = cache_k.reshape(depth, npool, PAGE, KV_WIDTH)
    cv = cache_v.reshape(depth, npool, PAGE, KV_WIDTH)

    outs = [[] for _ in range(12)]
    for l in range(depth):
        lw = _layer_weights(l, norms, w_in, ln_v_g, ln_v_b, w_spatial, b_spatial, w_branch_a, w_branch_b, w_out,
                            w_mem_q, w_mem_kv, w_mem_o, w_up, conv_w, conv_b, w_down)
        mkf, mvf, mkb, mvb = _memkv(mem2d, lw["wmkv"])
        a, vrow, q, kf, kb16, vf, vb16, qi, kw, kwb, sga, sgb = _in_proj(yp, lw, tabs_p, seq, CHUNK)
        o = _dsa_prompt(q, qi, kw, kwb, kb16, vb16, nb, seq)
        yp = _mix_out(a, o, sga, sgb, yp, lw)
        yp = _cross_prompt(yp, mkb, mvb, lw, nb, seq)
        yp, cp = _ffn_seq(yp, zero_state, lw, nb, seq)
        outs[0].append(kf.reshape(nb, seq, N_KV, HEAD_DIM))
        outs[1].append(vf.reshape(nb, seq, N_KV, HEAD_DIM))
        outs[2].append(kw[:, :IDX_DIM].reshape(nb, seq, IDX_DIM))
        outs[6].append(mkf.reshape(nb, mt, MEM_HEADS, MEM_HEAD_DIM))
        outs[7].append(mvf.reshape(nb, mt, MEM_HEADS, MEM_HEAD_DIM))
        outs[8].append(vrow)
        outs[10].append(cp)
        a, vrow, q, kf, kb16, vf, vb16, qi, kw, kwb, sga, sgb = _in_proj(ys, lw, tabs_s, 1, 1)
        o = _dsa_sample(q, qi, kw, kwb, kf, vf, cache_idx_k[l], ck[l], cv[l], page_table)
        ys = _mix_out(a, o, sga, sgb, ys, lw)
        ys = _cross_sample(ys, cache_mem_k[l], cache_mem_v[l], lw)
        ys, cs = _ffn_step(ys, state_conv[l], lw)
        outs[3].append(kf.reshape(nd, 1, N_KV, HEAD_DIM))
        outs[4].append(vf.reshape(nd, 1, N_KV, HEAD_DIM))
        outs[5].append(kw[:, :IDX_DIM].reshape(nd, 1, IDX_DIM))
        outs[9].append(vrow.reshape(nd, 1, A_WIDTH))
        outs[11].append(cs)
    return (yp.reshape(nb, seq, D_MODEL), ys.reshape(nd, 1, D_MODEL)) + tuple(jnp.stack(o) for o in outs)
```

```python
import functools

import numpy as np
import jax
import jax.numpy as jnp
from jax import lax
from jax.experimental import pallas as pl
from jax.experimental.pallas import tpu as pltpu

F32 = jnp.float32
BF16 = jnp.bfloat16
I32 = jnp.int32

D_MODEL = 1024
PAGE = 128
CHUNK = 128
A_GROUPS = 4
A_GROUP_DIM = 128
A_WIDTH = A_GROUPS * A_GROUP_DIM
N_HEADS = 8
N_KV = 4
HEAD_DIM = 128
GQA = N_HEADS // N_KV
KV_WIDTH = N_KV * HEAD_DIM
IDX_HEADS = 8
IDX_DIM = 64
TOPK_MAX = 256
ROPE_THETA = 10000.0
MEM_HEADS = 4
MEM_HEAD_DIM = 128
MEM_WIDTH = MEM_HEADS * MEM_HEAD_DIM
D_FF = 2816
CONV_WIDTH = 3
RMS_EPS = 1e-6
LN_EPS = 1e-5
IN_SIZES = (A_WIDTH, A_WIDTH, N_HEADS * HEAD_DIM, KV_WIDTH, KV_WIDTH,
            IDX_HEADS * IDX_DIM, IDX_DIM, IDX_HEADS, D_MODEL, D_MODEL)

LANES = 128
ROW_TILE = 512
QUERY_BLOCK = 128
KEY_BLOCK = 512
FF_CHUNK = 256
VMEM_LIMIT = 56 * 1024 * 1024
INT_MIN = -2147483648
NEG_BIAS = -1e30


def _cparams(sem):
    return pltpu.CompilerParams(dimension_semantics=sem, vmem_limit_bytes=VMEM_LIMIT)


def _const_spec(shape):
    nd = len(shape)
    return pl.BlockSpec(shape, lambda *_: (0,) * nd, pipeline_mode=pl.Buffered(1))


def _gelu(x):
    return 0.5 * x * (1.0 + jnp.tanh(0.7978845608028654 * (x + 0.044715 * (x * x * x))))


def _rms(x, g):
    return x * lax.rsqrt(jnp.mean(x * x, axis=-1, keepdims=True) + RMS_EPS) * g


def _dot(a, b):
    return jnp.dot(a, b, preferred_element_type=F32)


def _dot_t(a, b):
    return lax.dot_general(a, b, (((1,), (1,)), ((), ())), preferred_element_type=F32)


def _float_key(x):
    b = pltpu.bitcast(x, I32)
    return jnp.where(b < 0, b ^ jnp.int32(0x7FFFFFFF), b)


def _memkv_kernel(x_ref, w_ref, kf_ref, vf_ref, kb_ref, vb_ref):
    z = _dot(x_ref[...].astype(BF16), w_ref[...])
    k = z[:, :MEM_WIDTH]
    v = z[:, MEM_WIDTH:]
    kf_ref[...] = k
    vf_ref[...] = v
    kb_ref[...] = k.astype(BF16)
    vb_ref[...] = v.astype(BF16)


def _memkv(mem2d, w_bf):
    m = mem2d.shape[0]
    tm = min(ROW_TILE, m)
    out = [jax.ShapeDtypeStruct((m, MEM_WIDTH), F32)] * 2 + [jax.ShapeDtypeStruct((m, MEM_WIDTH), BF16)] * 2
    row = lambda w: pl.BlockSpec((tm, w), lambda i: (i, 0))
    return pl.pallas_call(
        _memkv_kernel,
        grid=(m // tm,),
        in_specs=[row(D_MODEL), _const_spec((D_MODEL, 2 * MEM_WIDTH))],
        out_specs=[row(MEM_WIDTH)] * 4,
        out_shape=out,
        compiler_params=_cparams(("parallel",)),
        name="memkv",
    )(mem2d, w_bf)


def _rope_full(z, cos, sin_signed):
    return z * cos + pltpu.roll(z, HEAD_DIM // 2, axis=1) * sin_signed


def _rope_idx(z, cos, sin_signed, lane):
    partner = jnp.where((lane % IDX_DIM) < IDX_DIM // 2,
                        pltpu.roll(z, LANES - IDX_DIM // 2, axis=1),
                        pltpu.roll(z, IDX_DIM // 2, axis=1))
    return z * cos + partner * sin_signed


def _in_proj_kernel(chunk, x_ref, g_ref, wu, wv, wq, wk, wvv, wqi, wkw, wga, wgb, lng, lnb, wsp, bsp,
                    c128, s128, c64, s64,
                    a_ref, vrow_ref, q_ref, kf_ref, kb_ref, vf_ref, vb_ref, qi_ref, kw_ref, kwb_ref,
                    sga_ref, sgb_ref):
    tm = x_ref.shape[0]
    h = _rms(x_ref[...], g_ref[...]).astype(BF16)
    lane = lax.broadcasted_iota(I32, (tm, LANES), 1)

    gu = _gelu(_dot(h, wu[...]))
    gv = _gelu(_dot(h, wv[...]))
    mu = jnp.mean(gv, axis=-1, keepdims=True)
    vc = gv - mu
    var = jnp.mean(vc * vc, axis=-1, keepdims=True)
    vln = vc * lax.rsqrt(var + LN_EPS) * lng[...] + lnb[...]
    if chunk == 1:
        vrow_ref[...] = vln
        a_ref[...] = (gu * (vln * wsp[...] + bsp[...])).astype(BF16)
    else:
        vrow_ref[0] = vln[tm - chunk:, :]
        tri = (lax.broadcasted_iota(I32, (chunk, chunk), 0) >= lax.broadcasted_iota(I32, (chunk, chunk), 1))
        vlb = vln.astype(BF16)
        for g in range(A_GROUPS):
            wt = jnp.where(tri, wsp[g], 0.0).astype(BF16)
            cs = slice(g * A_GROUP_DIM, (g + 1) * A_GROUP_DIM)
            for c in range(tm // chunk):
                rs = slice(c * chunk, (c + 1) * chunk)
                sv = _dot(wt, vlb[rs, cs]) + bsp[:, cs]
                a_ref[rs, cs] = (gu[rs, cs] * sv).astype(BF16)

    cos = c128[...]
    sin = s128[...]
    zq = _dot(h, wq[...])
    for hh in range(N_HEADS):
        cs = slice(hh * HEAD_DIM, (hh + 1) * HEAD_DIM)
        q_ref[:, cs] = (_rope_full(zq[:, cs], cos, sin) * (HEAD_DIM ** -0.5)).astype(BF16)
    zk = _dot(h, wk[...])
    for hh in range(N_KV):
        cs = slice(hh * HEAD_DIM, (hh + 1) * HEAD_DIM)
        kr = _rope_full(zk[:, cs], cos, sin)
        kf_ref[:, cs] = kr
        kb_ref[:, cs] = kr.astype(BF16)
    zv = _dot(h, wvv[...])
    vf_ref[...] = zv
    vb_ref[...] = zv.astype(BF16)

    ci = c64[...]
    si = s64[...]
    zqi = _dot(h, wqi[...])
    low = lane < IDX_DIM
    for p in range(IDX_HEADS // 2):
        r = _rope_idx(zqi[:, p * LANES:(p + 1) * LANES], ci, si, lane)
        qi_ref[:, (2 * p) * LANES:(2 * p + 1) * LANES] = jnp.where(low, r, 0.0).astype(BF16)
        qi_ref[:, (2 * p + 1) * LANES:(2 * p + 2) * LANES] = jnp.where(
            low, pltpu.roll(r, IDX_DIM, axis=1), 0.0).astype(BF16)
    zkw = _dot(h, wkw[...])
    kir = _rope_idx(zkw, ci, si, lane)
    wscale = (IDX_HEADS ** -0.5) * (IDX_DIM ** -0.5)
    kw_ref[...] = jnp.where(low, kir, zkw * wscale)
    kwb_ref[...] = jnp.where(low, kir, 0.0).astype(BF16)

    sga_ref[...] = jax.nn.sigmoid(_dot(h, wga[...])).astype(BF16)
    sgb_ref[...] = jax.nn.sigmoid(_dot(h, wgb[...])).astype(BF16)


def _in_proj(x2d, lw, tabs, seq, chunk):
    m = x2d.shape[0]
    tm = min(ROW_TILE, seq) if chunk != 1 else m
    nt = seq // tm if chunk != 1 else 1
    nb = m // seq if chunk != 1 else 1
    row = lambda w: pl.BlockSpec((tm, w), lambda i: (i, 0))
    tab = pl.BlockSpec((tm, LANES), lambda i: (i % nt, 0))
    if chunk == 1:
        vrow_spec = row(A_WIDTH)
        vrow_shape = jax.ShapeDtypeStruct((m, A_WIDTH), F32)
        sp_specs = [_const_spec((1, A_WIDTH)), _const_spec((1, A_WIDTH))]
    else:
        vrow_spec = pl.BlockSpec((1, chunk, A_WIDTH), lambda i: (i // nt, 0, 0))
        vrow_shape = jax.ShapeDtypeStruct((nb, chunk, A_WIDTH), F32)
        sp_specs = [_const_spec((A_GROUPS, chunk, chunk)), _const_spec((chunk, A_WIDTH))]
    widths = [(A_WIDTH, BF16), None, (N_HEADS * HEAD_DIM, BF16), (KV_WIDTH, F32), (KV_WIDTH, BF16),
              (KV_WIDTH, F32), (KV_WIDTH, BF16), (IDX_HEADS * LANES, BF16), (LANES, F32), (LANES, BF16),
              (D_MODEL, BF16), (D_MODEL, BF16)]
    out_specs = [vrow_spec if w is None else row(w[0]) for w in widths]
    out_shape = [vrow_shape if w is None else jax.ShapeDtypeStruct((m, w[0]), w[1]) for w in widths]
    wshapes = [(D_MODEL, A_WIDTH), (D_MODEL, A_WIDTH), (D_MODEL, N_HEADS * HEAD_DIM), (D_MODEL, KV_WIDTH),
               (D_MODEL, KV_WIDTH), (D_MODEL, IDX_HEADS * IDX_DIM), (D_MODEL, LANES), (D_MODEL, D_MODEL),
               (D_MODEL, D_MODEL)]
    in_specs = ([row(D_MODEL), _const_spec((1, D_MODEL))] + [_const_spec(s) for s in wshapes]
                + [_const_spec((1, A_WIDTH)), _const_spec((1, A_WIDTH))] + sp_specs + [tab] * 4)
    sp = (lw["sp_w1"], lw["sp_b1"]) if chunk == 1 else (lw["sp_w"], lw["sp_b"])
    return pl.pallas_call(
        functools.partial(_in_proj_kernel, chunk),
        grid=(m // tm,),
        in_specs=in_specs,
        out_specs=out_specs,
        out_shape=out_shape,
        compiler_params=_cparams(("arbitrary",)),
        name="in_proj",
    )(x2d, lw["g0"], lw["wu"], lw["wv"], lw["wq"], lw["wk"], lw["wvv"], lw["wqi"], lw["wkw"], lw["wga"],
      lw["wgb"], lw["ln_g"], lw["ln_b"], sp[0], sp[1], *tabs)


def _topk_threshold(count_ge, topk, shape):
    t0 = jnp.where(count_ge(jnp.zeros(shape, I32)) >= topk, jnp.int32(0), jnp.int32(INT_MIN))

    def bit_body(b, t):
        cand = t + lax.shift_left(jnp.int32(1), jnp.int32(30) - b)
        return jnp.where(count_ge(cand) >= topk, cand, t)

    return lax.fori_loop(0, 31, bit_body, jnp.broadcast_to(t0, shape))


def _dsa_prompt_kernel(topk, kb, q_ref, qi_ref, kw_ref, kib_ref, k_ref, v_ref, o_ref, qis, wib, keys, bias,
                       qn, m_ref, l_ref, acc_ref):
    tq = q_ref.shape[1]
    seq = k_ref.shape[1]
    i = pl.program_id(1)
    nkb = ((i + 1) * tq + kb - 1) // kb
    ncol = kb // LANES
    lane = lax.broadcasted_iota(I32, (tq, LANES), 1)
    rowpos = i * tq + lax.broadcasted_iota(I32, (tq, LANES), 0)
    shape = (tq, LANES)

    for h in range(IDX_HEADS):
        qis[h * tq:(h + 1) * tq, :] = qi_ref[0, :, h * LANES:(h + 1) * LANES]
        wib[h] = jnp.broadcast_to(kw_ref[0, :, IDX_DIM + h:IDX_DIM + h + 1], shape)

    def score_body(j, carry):
        base = pl.multiple_of(j * kb, kb)
        lg = _dot_t(qis[...], kib_ref[0, pl.ds(base, kb), :])
        for c in range(ncol):
            acc = jnp.zeros(shape, F32)
            for h in range(IDX_HEADS):
                acc = acc + wib[h] * jnp.maximum(lg[h * tq:(h + 1) * tq, c * LANES:(c + 1) * LANES], 0.0)
            col = base + c * LANES + lane
            key = jnp.where(col <= rowpos, _float_key(acc), jnp.int32(INT_MIN))
            keys[:, pl.ds(pl.multiple_of(base + c * LANES, LANES), LANES)] = key
        return carry

    lax.fori_loop(0, nkb, score_body, 0)

    def count(pred):
        def body(j, part):
            base = pl.multiple_of(j * kb, kb)
            blk = keys[:, pl.ds(base, kb)]
            for c in range(ncol):
                col = base + c * LANES + lane
                part = part + jnp.where(pred(blk[:, c * LANES:(c + 1) * LANES], col), 1.0, 0.0)
            return part
        part = lax.fori_loop(0, nkb, body, jnp.zeros(shape, F32))
        return jnp.broadcast_to(jnp.sum(part, axis=1, keepdims=True), shape)

    thr = _topk_threshold(lambda cand: count(lambda k, col: k >= cand), float(topk), shape)
    thr = jnp.maximum(thr, jnp.int32(INT_MIN + 1))

    def write_bias(sel):
        def body(j, carry):
            base = pl.multiple_of(j * kb, kb)
            blk = keys[:, pl.ds(base, kb)]
            for c in range(ncol):
                col = base + c * LANES + lane
                bias[:, pl.ds(pl.multiple_of(base + c * LANES, LANES), LANES)] = jnp.where(
                    sel(blk[:, c * LANES:(c + 1) * LANES], col), 0.0, NEG_BIAS)
            return carry
        lax.fori_loop(0, nkb, body, 0)

    n_ge = count(lambda k, col: k >= thr)
    has_ties = jnp.max(n_ge) > float(topk)

    @pl.when(jnp.logical_not(has_ties))
    def _():
        write_bias(lambda k, col: k >= thr)

    @pl.when(has_ties)
    def _():
        need = float(topk) - count(lambda k, col: k > thr)
        nbits = int(np.ceil(np.log2(seq))) + 1

        def bit_body(b, p):
            cand = p + lax.shift_left(jnp.int32(1), jnp.int32(nbits - 1) - b)
            below = count(lambda k, col: jnp.logical_and(k == thr, col < cand))
            return jnp.where(below < need, cand, p)

        last = lax.fori_loop(0, nbits, bit_body, jnp.zeros(shape, I32))
        write_bias(lambda k, col: jnp.logical_or(k > thr, jnp.logical_and(k == thr, col <= last)))

    for n in range(N_KV):
        for g in range(GQA):
            hh = GQA * n + g
            qn[n, g * tq:(g + 1) * tq, :] = q_ref[0, :, hh * HEAD_DIM:(hh + 1) * HEAD_DIM]
    m_ref[...] = jnp.full(m_ref.shape, -jnp.inf, F32)
    l_ref[...] = jnp.zeros(l_ref.shape, F32)
    acc_ref[...] = jnp.zeros(acc_ref.shape, F32)

    def att_body(j, carry):
        base = pl.multiple_of(j * kb, kb)
        bb = bias[:, pl.ds(base, kb)]
        bb = jnp.concatenate([bb] * GQA, axis=0)
        for n in range(N_KV):
            cs = slice(n * HEAD_DIM, (n + 1) * HEAD_DIM)
            s = _dot_t(qn[n], k_ref[0, pl.ds(base, kb), cs]) + bb
            m_old = m_ref[n]
            m_new = jnp.maximum(m_old, jnp.max(s, axis=1, keepdims=True))
            p = jnp.exp(s - m_new)
            alpha = jnp.exp(m_old - m_new)
            l_ref[n] = alpha * l_ref[n] + jnp.sum(p, axis=1, keepdims=True)
            acc_ref[n] = alpha * acc_ref[n] + _dot(p.astype(BF16), v_ref[0, pl.ds(base, kb), cs])
            m_ref[n] = m_new
        return carry

    lax.fori_loop(0, nkb, att_body, 0)
    for n in range(N_KV):
        o = acc_ref[n] / l_ref[n]
        for g in range(GQA):
            hh = GQA * n + g
            o_ref[0, :, hh * HEAD_DIM:(hh + 1) * HEAD_DIM] = o[g * tq:(g + 1) * tq].astype(BF16)


def _dsa_prompt(q, qi, kw, kib, kb16, vb16, nb, seq):
    tq = min(QUERY_BLOCK, seq)
    kb = min(KEY_BLOCK, seq)
    topk = min(TOPK_MAX, seq // 4)
    r3 = lambda a: a.reshape(nb, seq, a.shape[-1])
    qblk = lambda w: pl.BlockSpec((1, tq, w), lambda b, i: (b, i, 0))
    full = lambda w: pl.BlockSpec((1, seq, w), lambda b, i: (b, 0, 0))
    out = pl.pallas_call(
        functools.partial(_dsa_prompt_kernel, topk, kb),
        grid=(nb, seq // tq),
        in_specs=[qblk(N_HEADS * HEAD_DIM), qblk(IDX_HEADS * LANES), qblk(LANES), full(LANES), full(KV_WIDTH),
                  full(KV_WIDTH)],
        out_specs=qblk(N_HEADS * HEAD_DIM),
        out_shape=jax.ShapeDtypeStruct((nb, seq, N_HEADS * HEAD_DIM), BF16),
        scratch_shapes=[pltpu.VMEM((IDX_HEADS * tq, LANES), BF16), pltpu.VMEM((IDX_HEADS, tq, LANES), F32),
                        pltpu.VMEM((tq, seq), I32), pltpu.VMEM((tq, seq), F32),
                        pltpu.VMEM((N_KV, GQA * tq, HEAD_DIM), BF16), pltpu.VMEM((N_KV, GQA * tq, 1), F32),
                        pltpu.VMEM((N_KV, GQA * tq, 1), F32), pltpu.VMEM((N_KV, GQA * tq, HEAD_DIM), F32)],
        compiler_params=_cparams(("parallel", "arbitrary")),
        name="dsa_prompt",
    )(r3(q), r3(qi), r3(kw), r3(kib), r3(kb16), r3(vb16))
    return out.reshape(nb * seq, N_HEADS * HEAD_DIM)


def _dsa_select_kernel(layer, topk, npages, pt_ref, qi_ref, kw_ref, kwb_ref, cik_ref, idx_ref, ibuf, keys, grank, isem):
    b = pl.program_id(0)
    nrow = keys.shape[0]
    group = 8

    def idx_copy(j):
        return pltpu.make_async_copy(cik_ref.at[layer, pt_ref[b, j]], ibuf.at[j], isem)

    def start_idx(j, c):
        idx_copy(j).start()
        return c

    lax.fori_loop(0, npages, start_idx, 0)

    qif = qi_ref[0].astype(F32)
    qi8f = jnp.concatenate([qif[:, h * LANES:h * LANES + IDX_DIM] for h in range(IDX_HEADS)], axis=0)
    qi8 = qi8f.astype(BF16)
    kwrow = kw_ref[0]
    eye = (lax.broadcasted_iota(I32, (IDX_HEADS, LANES), 0) + IDX_DIM
           == lax.broadcasted_iota(I32, (IDX_HEADS, LANES), 1))
    wcol = jnp.sum(jnp.where(eye, jnp.broadcast_to(kwrow, (IDX_HEADS, LANES)), 0.0), axis=1, keepdims=True)

    def wait_idx(j, c):
        idx_copy(j).wait()
        return c

    lax.fori_loop(0, npages, wait_idx, 0)

    def score_body(jj, c):
        rows = []
        for u in range(group):
            lg = _dot_t(qi8, ibuf[jj * group + u].astype(BF16))
            rows.append(jnp.sum(wcol * jnp.maximum(lg, 0.0), axis=0, keepdims=True))
        tile = jnp.concatenate(rows, axis=0)
        keys[pl.ds(pl.multiple_of(jj * group, group), group), :] = _float_key(tile + 0.0)
        return c

    lax.fori_loop(0, npages // group, score_body, 0)
    lgs = jnp.sum(qi8f * kwb_ref[0].astype(F32)[:, :IDX_DIM], axis=1, keepdims=True)
    scs = jnp.sum(wcol * jnp.maximum(lgs, 0.0), axis=0, keepdims=True)
    tshape = (nrow - npages, LANES)
    first = jnp.logical_and(lax.broadcasted_iota(I32, tshape, 0) == 0, lax.broadcasted_iota(I32, tshape, 1) == 0)
    keys[npages:nrow, :] = jnp.where(first, _float_key(jnp.broadcast_to(scs, tshape) + 0.0), jnp.int32(INT_MIN))

    kall = keys[...]
    pos = lax.broadcasted_iota(I32, (nrow, LANES), 0) * LANES + lax.broadcasted_iota(I32, (nrow, LANES), 1)
    shape = (1, 1)

    def count(pred):
        hit = jnp.where(pred, 1.0, 0.0)
        return jnp.sum(jnp.sum(hit, axis=1, keepdims=True), axis=0, keepdims=True)

    thr = _topk_threshold(lambda cand: count(kall >= cand), float(topk), shape)
    thr = jnp.maximum(thr, jnp.int32(INT_MIN + 1))
    need = float(topk) - count(kall > thr)
    tied = kall == thr
    nbits = int(np.ceil(np.log2(nrow * LANES))) + 1

    def bit_body(bb, p):
        cand = p + lax.shift_left(jnp.int32(1), jnp.int32(nbits - 1) - bb)
        below = count(jnp.logical_and(tied, pos < cand))
        return jnp.where(below < need, cand, p)

    last = lax.fori_loop(0, nbits, bit_body, jnp.zeros(shape, I32))
    sel = jnp.logical_or(kall > thr, jnp.logical_and(tied, pos <= last))
    msk = jnp.where(sel, 1.0, 0.0)

    upper = (lax.broadcasted_iota(I32, (LANES, LANES), 0) <= lax.broadcasted_iota(I32, (LANES, LANES), 1))
    within = _dot(msk.astype(BF16), jnp.where(upper, 1.0, 0.0).astype(BF16))
    rowsum = jnp.broadcast_to(jnp.sum(msk, axis=1, keepdims=True), (nrow, LANES))
    before = (lax.broadcasted_iota(I32, (nrow, nrow), 1) < lax.broadcasted_iota(I32, (nrow, nrow), 0))
    offset = _dot(jnp.where(before, 1.0, 0.0).astype(BF16), rowsum.astype(BF16))
    grank[...] = jnp.where(sel, offset + within - 1.0, -1.0)

    slot = lax.broadcasted_iota(I32, (topk, LANES), 0).astype(F32)
    lane_f = lax.broadcasted_iota(I32, (1, LANES), 1).astype(F32)

    def compact_body(r, acc):
        g = jnp.broadcast_to(grank[pl.ds(r, 1), :], (topk, LANES))
        flat = jnp.broadcast_to(lax.convert_element_type(r * LANES, F32) + lane_f, (topk, LANES))
        return acc + jnp.where(g == slot, flat, 0.0)

    acc = lax.fori_loop(0, npages + 1, compact_body, jnp.zeros((topk, LANES), F32))
    idx_ref[0] = jnp.sum(acc, axis=1, keepdims=True).astype(I32)


def _dsa_attend_kernel(layer, topk, npages, pt_ref, ix_ref, q_ref, kn_ref, vn_ref, ck_ref, cv_ref, o_ref,
                       kg, vg, ksem, vsem):
    b = pl.program_id(0)
    past = npages * PAGE

    def issue(t, c):
        ix = ix_ref[b, t]

        @pl.when(ix < past)
        def _():
            pg = pt_ref[b, lax.shift_right_logical(ix, 7)]
            off = jnp.bitwise_and(ix, PAGE - 1)
            pltpu.make_async_copy(ck_ref.at[layer, pg, off], kg.at[t], ksem).start()
            pltpu.make_async_copy(cv_ref.at[layer, pg, off], vg.at[t], vsem).start()

        @pl.when(ix >= past)
        def _():
            pltpu.make_async_copy(kn_ref.at[b], kg.at[t], ksem).start()
            pltpu.make_async_copy(vn_ref.at[b], vg.at[t], vsem).start()

        return c

    lax.fori_loop(0, topk, issue, 0)

    def wait(t, c):
        pltpu.make_async_copy(kn_ref.at[b], kg.at[t], ksem).wait()
        pltpu.make_async_copy(vn_ref.at[b], vg.at[t], vsem).wait()
        return c

    lax.fori_loop(0, topk, wait, 0)

    qf = q_ref[0].astype(F32)
    pad = jnp.zeros((8 - GQA, HEAD_DIM), F32)
    for n in range(N_KV):
        rows = [qf[:, (GQA * n + g) * HEAD_DIM:(GQA * n + g + 1) * HEAD_DIM] for g in range(GQA)]
        q8 = jnp.concatenate(rows + [pad], axis=0).astype(BF16)
        s = _dot_t(q8, kg[:, n, :].astype(BF16))
        p = jnp.exp(s - jnp.max(s, axis=1, keepdims=True))
        p = p / jnp.sum(p, axis=1, keepdims=True)
        o = _dot(p.astype(BF16), vg[:, n, :].astype(BF16))
        for g in range(GQA):
            hh = GQA * n + g
            o_ref[0, :, hh * HEAD_DIM:(hh + 1) * HEAD_DIM] = o[g:g + 1, :].astype(BF16)


def _dsa_sample(layer, q, qi, kw, kwb, kf, vf, cache_idx_k, cache_k, cache_v, page_table):
    nb, npages = page_table.shape
    assert npages % 8 == 0 and PAGE == 128
    topk = min(TOPK_MAX, (npages * PAGE + 1) // 4)
    nrow = -(-(npages + 1) // LANES) * LANES
    r3 = lambda a: a.reshape(nb, 1, a.shape[-1])
    anyspec = pl.BlockSpec(memory_space=pl.ANY)
    blk1 = lambda w: pl.BlockSpec((1, 1, w), lambda b, pt: (b, 0, 0))
    idx = pl.pallas_call(
        functools.partial(_dsa_select_kernel, layer, topk, npages),
        grid_spec=pltpu.PrefetchScalarGridSpec(
            num_scalar_prefetch=1,
            grid=(nb,),
            in_specs=[blk1(IDX_HEADS * LANES), blk1(LANES), blk1(LANES), anyspec],
            out_specs=pl.BlockSpec((1, topk, 1), lambda b, pt: (b, 0, 0)),
            scratch_shapes=[pltpu.VMEM((npages, PAGE, IDX_DIM), F32), pltpu.VMEM((nrow, LANES), I32),
                            pltpu.VMEM((nrow, LANES), F32), pltpu.SemaphoreType.DMA(())],
        ),
        out_shape=jax.ShapeDtypeStruct((nb, topk, 1), I32),
        compiler_params=_cparams(("arbitrary",)),
        name="dsa_select",
    )(page_table, r3(qi), r3(kw), r3(kwb), cache_idx_k)
    blk2 = lambda w: pl.BlockSpec((1, 1, w), lambda b, pt, ix: (b, 0, 0))
    rowshape = (topk, N_KV, HEAD_DIM)
    out = pl.pallas_call(
        functools.partial(_dsa_attend_kernel, layer, topk, npages),
        grid_spec=pltpu.PrefetchScalarGridSpec(
            num_scalar_prefetch=2,
            grid=(nb,),
            in_specs=[blk2(N_HEADS * HEAD_DIM), anyspec, anyspec, anyspec, anyspec],
            out_specs=blk2(N_HEADS * HEAD_DIM),
            scratch_shapes=[pltpu.VMEM(rowshape, F32), pltpu.VMEM(rowshape, F32),
                            pltpu.SemaphoreType.DMA(()), pltpu.SemaphoreType.DMA(())],
        ),
        out_shape=jax.ShapeDtypeStruct((nb, 1, N_HEADS * HEAD_DIM), BF16),
        compiler_params=_cparams(("arbitrary",)),
        name="dsa_attend",
    )(page_table, idx.reshape(nb, topk), r3(q), kf.reshape(nb, N_KV, HEAD_DIM), vf.reshape(nb, N_KV, HEAD_DIM),
      cache_k, cache_v)
    return out.reshape(nb, N_HEADS * HEAD_DIM)


def _mix_out_kernel(a_ref, o_ref, sga_ref, sgb_ref, x_ref, wba, wbb, wout, g_ref, y_ref):
    merged = (sga_ref[...].astype(F32) * _dot(a_ref[...], wba[...])
              + sgb_ref[...].astype(F32) * _dot(o_ref[...], wbb[...]))
    mix = _dot(merged.astype(BF16), wout[...])
    y_ref[...] = x_ref[...] + _rms(mix, g_ref[...])


def _mix_out(a, o, sga, sgb, x2d, lw):
    m = x2d.shape[0]
    tm = min(ROW_TILE, m)
    row = lambda w: pl.BlockSpec((tm, w), lambda i: (i, 0))
    return pl.pallas_call(
        _mix_out_kernel,
        grid=(m // tm,),
        in_specs=[row(A_WIDTH), row(D_MODEL), row(D_MODEL), row(D_MODEL), row(D_MODEL),
                  _const_spec((A_WIDTH, D_MODEL)), _const_spec((D_MODEL, D_MODEL)), _const_spec((D_MODEL, D_MODEL)),
                  _const_spec((1, D_MODEL))],
        out_specs=row(D_MODEL),
        out_shape=jax.ShapeDtypeStruct((m, D_MODEL), F32),
        compiler_params=_cparams(("parallel",)),
        name="mix_out",
    )(a, o, sga, sgb, x2d, lw["wba"], lw["wbb"], lw["wout"], lw["g1"])


def _cross_prompt_kernel(x_ref, g2, g3, wq, wo, mk_ref, mv_ref, y_ref):
    x = x_ref[0]
    h = _rms(x, g2[...]).astype(BF16)
    qm = (_dot(h, wq[...]) * (MEM_HEAD_DIM ** -0.5)).astype(BF16)
    outs = []
    for hh in range(MEM_HEADS):
        cs = slice(hh * MEM_HEAD_DIM, (hh + 1) * MEM_HEAD_DIM)
        s = _dot_t(qm[:, cs], mk_ref[0, :, cs])
        p = jnp.exp(s - jnp.max(s, axis=1, keepdims=True))
        p = p / jnp.sum(p, axis=1, keepdims=True)
        outs.append(_dot(p.astype(BF16), mv_ref[0, :, cs]).astype(BF16))
    c = _dot(jnp.concatenate(outs, axis=1), wo[...])
    y_ref[0] = x + _rms(c, g3[...])


def _cross_prompt(x2d, mk, mv, lw, nb, seq):
    tm = min(ROW_TILE, seq)
    mt = mk.shape[0] // nb
    xb = pl.BlockSpec((1, tm, D_MODEL), lambda b, i: (b, i, 0))
    mb = pl.BlockSpec((1, mt, MEM_WIDTH), lambda b, i: (b, 0, 0))
    out = pl.pallas_call(
        _cross_prompt_kernel,
        grid=(nb, seq // tm),
        in_specs=[xb, _const_spec((1, D_MODEL)), _const_spec((1, D_MODEL)), _const_spec((D_MODEL, MEM_WIDTH)),
                  _const_spec((MEM_WIDTH, D_MODEL)), mb, mb],
        out_specs=xb,
        out_shape=jax.ShapeDtypeStruct((nb, seq, D_MODEL), F32),
        compiler_params=_cparams(("parallel", "parallel")),
        name="cross_prompt",
    )(x2d.reshape(nb, seq, D_MODEL), lw["g2"], lw["g3"], lw["wmq"], lw["wmo"],
      mk.reshape(nb, mt, MEM_WIDTH), mv.reshape(nb, mt, MEM_WIDTH))
    return out.reshape(nb * seq, D_MODEL)


def _cross_sample_kernel(x_ref, g2, g3, wq, wo, mk_ref, mv_ref, y_ref):
    x = x_ref[...]
    bt = x.shape[0]
    h = _rms(x, g2[...]).astype(BF16)
    qm = (_dot(h, wq[...]) * (MEM_HEAD_DIM ** -0.5)).astype(BF16)
    rows = []
    for b in range(bt):
        qb = jnp.broadcast_to(qm[b:b + 1, :], (8, MEM_WIDTH))
        outs = []
        for hh in range(MEM_HEADS):
            cs = slice(hh * MEM_HEAD_DIM, (hh + 1) * MEM_HEAD_DIM)
            s = _dot_t(qb[:, cs], mk_ref[0, b, :, hh, :].astype(BF16))
            p = jnp.exp(s - jnp.max(s, axis=1, keepdims=True))
            p = p / jnp.sum(p, axis=1, keepdims=True)
            outs.append(_dot(p.astype(BF16), mv_ref[0, b, :, hh, :].astype(BF16)))
        rows.append(jnp.concatenate(outs, axis=1)[0:1, :])
    att = jnp.concatenate(rows, axis=0).astype(BF16)
    c = _dot(att, wo[...])
    y_ref[...] = x + _rms(c, g3[...])


def _cross_sample(layer, x2d, mk, mv, lw):
    nb = x2d.shape[0]
    bt = 8
    mt = mk.shape[2]
    xb = pl.BlockSpec((bt, D_MODEL), lambda i: (i, 0))
    mb = pl.BlockSpec((1, bt, mt, MEM_HEADS, MEM_HEAD_DIM), lambda i: (layer, i, 0, 0, 0))
    return pl.pallas_call(
        _cross_sample_kernel,
        grid=(nb // bt,),
        in_specs=[xb, _const_spec((1, D_MODEL)), _const_spec((1, D_MODEL)), _const_spec((D_MODEL, MEM_WIDTH)),
                  _const_spec((MEM_WIDTH, D_MODEL)), mb, mb],
        out_specs=xb,
        out_shape=jax.ShapeDtypeStruct((nb, D_MODEL), F32),
        compiler_params=_cparams(("parallel",)),
        name="cross_sample",
    )(x2d, lw["g2"], lw["g3"], lw["wmq"], lw["wmo"], mk, mv)


def _ffn_seq_kernel(x_ref, g4, g5, wup, cw, cb, wdn, st_ref, y_ref, ns_ref, carry, ext, acc):
    tm = x_ref.shape[1]
    t = pl.program_id(1)

    @pl.when(t == 0)
    def _():
        carry[6:8, :] = st_ref[0]

    x = x_ref[0]
    h = _rms(x, g4[...]).astype(BF16)
    for c in range(D_FF // FF_CHUNK):
        prod = None
        for half in range(2):
            cs = slice(half * D_FF + c * FF_CHUNK, half * D_FF + (c + 1) * FF_CHUNK)
            up = _dot(h, wup[:, cs])
            ext[half, 0:8, :] = carry[:, cs]
            ext[half, 8:8 + tm, :] = up
            carry[:, cs] = up[tm - 8:, :]
            conv = (cb[:, cs] + cw[0:1, cs] * ext[half, 6:6 + tm, :] + cw[1:2, cs] * ext[half, 7:7 + tm, :]
                    + cw[2:3, cs] * up)
            prod = _gelu(conv) if half == 0 else prod * conv
        part = _dot(prod.astype(BF16), wdn[c * FF_CHUNK:(c + 1) * FF_CHUNK, :])
        if c == 0:
            acc[...] = part
        else:
            acc[...] += part
    y_ref[0] = x + _rms(acc[...], g5[...])
    ns_ref[0] = carry[6:8, :]


def _ffn_seq(x2d, state, lw, nb, seq):
    tm = min(ROW_TILE, seq)
    xb = pl.BlockSpec((1, tm, D_MODEL), lambda b, t: (b, t, 0))
    sb = pl.BlockSpec((1, CONV_WIDTH - 1, 2 * D_FF), lambda b, t: (b, 0, 0))
    y, ns = pl.pallas_call(
        _ffn_seq_kernel,
        grid=(nb, seq // tm),
        in_specs=[xb, _const_spec((1, D_MODEL)), _const_spec((1, D_MODEL)), _const_spec((D_MODEL, 2 * D_FF)),
                  _const_spec((CONV_WIDTH, 2 * D_FF)), _const_spec((1, 2 * D_FF)), _const_spec((D_FF, D_MODEL)), sb],
        out_specs=[xb, sb],
        out_shape=[jax.ShapeDtypeStruct((nb, seq, D_MODEL), F32),
                   jax.ShapeDtypeStruct((nb, CONV_WIDTH - 1, 2 * D_FF), F32)],
        scratch_shapes=[pltpu.VMEM((8, 2 * D_FF), F32), pltpu.VMEM((2, tm + 8, FF_CHUNK), F32),
                        pltpu.VMEM((tm, D_MODEL), F32)],
        compiler_params=_cparams(("arbitrary", "arbitrary")),
        name="ffn_seq",
    )(x2d.reshape(nb, seq, D_MODEL), lw["g4"], lw["g5"], lw["wup"], lw["cw"], lw["cb"], lw["wdn"], state)
    return y.reshape(nb * seq, D_MODEL), ns


def _ffn_step_kernel(x_ref, g4, g5, wup, cw, cb, wdn, s0_ref, s1_ref, y_ref, up_ref):
    x = x_ref[...]
    h = _rms(x, g4[...]).astype(BF16)
    acc = None
    for c in range(D_FF // FF_CHUNK):
        prod = None
        for half in range(2):
            cs = slice(half * D_FF + c * FF_CHUNK, half * D_FF + (c + 1) * FF_CHUNK)
            up = _dot(h, wup[:, cs])
            up_ref[:, cs] = up
            conv = cb[:, cs] + cw[0:1, cs] * s0_ref[:, cs] + cw[1:2, cs] * s1_ref[:, cs] + cw[2:3, cs] * up
            prod = _gelu(conv) if half == 0 else prod * conv
        part = _dot(prod.astype(BF16), wdn[c * FF_CHUNK:(c + 1) * FF_CHUNK, :])
        acc = part if acc is None else acc + part
    y_ref[...] = x + _rms(acc, g5[...])


def _ffn_step(x2d, state, lw):
    nb = x2d.shape[0]
    s0 = state[:, 0, :]
    s1 = state[:, 1, :]
    full = lambda w: pl.BlockSpec((nb, w), lambda i: (0, 0))
    y, up = pl.pallas_call(
        _ffn_step_kernel,
        grid=(1,),
        in_specs=[full(D_MODEL), _const_spec((1, D_MODEL)), _const_spec((1, D_MODEL)),
                  _const_spec((D_MODEL, 2 * D_FF)), _const_spec((CONV_WIDTH, 2 * D_FF)), _const_spec((1, 2 * D_FF)),
                  _const_spec((D_FF, D_MODEL)), full(2 * D_FF), full(2 * D_FF)],
        out_specs=[full(D_MODEL), full(2 * D_FF)],
        out_shape=[jax.ShapeDtypeStruct((nb, D_MODEL), F32), jax.ShapeDtypeStruct((nb, 2 * D_FF), F32)],
        compiler_params=_cparams(("arbitrary",)),
        name="ffn_step",
    )(x2d, lw["g4"], lw["g5"], lw["wup"], lw["cw"], lw["cb"], lw["wdn"], s0, s1)
    return y, jnp.stack([s1, up], axis=1)


def _rope_tables(pos):
    def tab(dim, reps):
        half = dim // 2
        inv = ROPE_THETA ** (-jnp.arange(half, dtype=F32) / half)
        ang = pos.astype(F32)[:, None] * inv[None, :]
        cos = jnp.concatenate([jnp.cos(ang)] * 2, axis=1)
        sin = jnp.concatenate([-jnp.sin(ang), jnp.sin(ang)], axis=1)
        return jnp.tile(cos, (1, reps)), jnp.tile(sin, (1, reps))
    c128, s128 = tab(HEAD_DIM, 1)
    c64, s64 = tab(IDX_DIM, LANES // IDX_DIM)
    return c128, s128, c64, s64


def _layer_weights(l, norms, w_in, ln_v_g, ln_v_b, w_spatial, b_spatial, w_branch_a, w_branch_b, w_out, w_mem_q,
                   w_mem_kv, w_mem_o, w_up, conv_w, conv_b, w_down):
    offs = np.cumsum((0,) + IN_SIZES)
    seg = lambda i: w_in[l][:, offs[i]:offs[i + 1]].astype(BF16)
    wkw = jnp.concatenate([w_in[l][:, offs[6]:offs[8]],
                           jnp.zeros((D_MODEL, LANES - IDX_DIM - IDX_HEADS), F32)], axis=1).astype(BF16)
    lw = dict(
        wu=seg(0), wv=seg(1), wq=seg(2), wk=seg(3), wvv=seg(4), wqi=seg(5), wkw=wkw, wga=seg(8), wgb=seg(9),
        ln_g=ln_v_g[l][None, :], ln_b=ln_v_b[l][None, :],
        sp_w=w_spatial[l], sp_b=jnp.repeat(b_spatial[l].T, A_GROUP_DIM, axis=1),
        sp_w1=jnp.repeat(w_spatial[l][:, 0, 0], A_GROUP_DIM)[None, :],
        sp_b1=jnp.repeat(b_spatial[l][:, 0], A_GROUP_DIM)[None, :],
        wba=w_branch_a[l].astype(BF16), wbb=w_branch_b[l].astype(BF16), wout=w_out[l].astype(BF16),
        wmq=w_mem_q[l].astype(BF16), wmkv=w_mem_kv[l].astype(BF16), wmo=w_mem_o[l].astype(BF16),
        wup=w_up[l].astype(BF16), cw=conv_w[l], cb=conv_b[l][None, :], wdn=w_down[l].astype(BF16),
    )
    for i in range(6):
        lw["g%d" % i] = norms[l, i][None, :]
    return lw


def kernel(x_prompt, x_sample, cache_k, cache_v, cache_idx_k, cache_mem_k, cache_mem_v, state_conv, page_table,
           mem_prompt, norms, w_in, ln_v_g, ln_v_b, w_spatial, b_spatial, w_branch_a, w_branch_b, w_out,
           w_mem_q, w_mem_kv, w_mem_o, w_up, conv_w, conv_b, w_down):
    nb, seq, _ = x_prompt.shape
    nd, tdec, _ = x_sample.shape
    assert tdec == 1 and seq % CHUNK == 0
    depth = norms.shape[0]
    mt = mem_prompt.shape[1]
    past = page_table.shape[1] * PAGE

    tabs_p = _rope_tables(jnp.arange(seq))
    tabs_s = tuple(jnp.broadcast_to(t, (nd, LANES)) for t in _rope_tables(jnp.full((1,), past)))
    yp = x_prompt.reshape(nb * seq, D_MODEL)
    ys = x_sample.reshape(nd, D_MODEL)
    mem2d = mem_prompt.reshape(nb * mt, D_MODEL)
    zero_state = jnp.zeros((nb, CONV_WIDTH - 1, 2 * D_FF), F32)

    outs = [[] for _ in range(12)]
    for l in range(depth):
        lw = _layer_weights(l, norms, w_in, ln_v_g, ln_v_b, w_spatial, b_spatial, w_branch_a, w_branch_b, w_out,
                            w_mem_q, w_mem_kv, w_mem_o, w_up, conv_w, conv_b, w_down)
        mkf, mvf, mkb, mvb = _memkv(mem2d, lw["wmkv"])
        a, vrow, q, kf, kb16, vf, vb16, qi, kw, kwb, sga, sgb = _in_proj(yp, lw, tabs_p, seq, CHUNK)
        o = _dsa_prompt(q, qi, kw, kwb, kb16, vb16, nb, seq)
        yp = _mix_out(a, o, sga, sgb, yp, lw)
        yp = _cross_prompt(yp, mkb, mvb, lw, nb, seq)
        yp, cp = _ffn_seq(yp, zero_state, lw, nb, seq)
        outs[0].append(kf.reshape(nb, seq, N_KV, HEAD_DIM))
        outs[1].append(vf.reshape(nb, seq, N_KV, HEAD_DIM))
        outs[2].append(kw[:, :IDX_DIM].reshape(nb, seq, IDX_DIM))
        outs[6].append(mkf.reshape(nb, mt, MEM_HEADS, MEM_HEAD_DIM))
        outs[7].append(mvf.reshape(nb, mt, MEM_HEADS, MEM_HEAD_DIM))
        outs[8].append(vrow)
        outs[10].append(cp)
        a, vrow, q, kf, kb16, vf, vb16, qi, kw, kwb, sga, sgb = _in_proj(ys, lw, tabs_s, 1, 1)
        o = _dsa_sample(l, q, qi, kw, kwb, kf, vf, cache_idx_k, cache_k, cache_v, page_table)
        ys = _mix_out(a, o, sga, sgb, ys, lw)
        ys = _cross_sample(l, ys, cache_mem_k, cache_mem_v, lw)
        ys, cs = _ffn_step(ys, state_conv[l], lw)
        outs[3].append(kf.reshape(nd, 1, N_KV, HEAD_DIM))
        outs[4].append(vf.reshape(nd, 1, N_KV, HEAD_DIM))
        outs[5].append(kw[:, :IDX_DIM].reshape(nd, 1, IDX_DIM))
        outs[9].append(vrow.reshape(nd, 1, A_WIDTH))
        outs[11].append(cs)
    return (yp.reshape(nb, seq, D_MODEL), ys.reshape(nd, 1, D_MODEL)) + tuple(jnp.stack(o) for o in outs)
```

```python
import functools

import numpy as np
import jax
import jax.numpy as jnp
from jax import lax
from jax.experimental import pallas as pl
from jax.experimental.pallas import tpu as pltpu

F32 = jnp.float32
BF16 = jnp.bfloat16
I32 = jnp.int32

D_MODEL = 1024
PAGE = 128
CHUNK = 128
A_GROUPS = 4
A_GROUP_DIM = 128
A_WIDTH = A_GROUPS * A_GROUP_DIM
N_HEADS = 8
N_KV = 4
HEAD_DIM = 128
GQA = N_HEADS // N_KV
KV_WIDTH = N_KV * HEAD_DIM
IDX_HEADS = 8
IDX_DIM = 64
TOPK_MAX = 256
ROPE_THETA = 10000.0
MEM_HEADS = 4
MEM_HEAD_DIM = 128
MEM_WIDTH = MEM_HEADS * MEM_HEAD_DIM
D_FF = 2816
CONV_WIDTH = 3
RMS_EPS = 1e-6
LN_EPS = 1e-5
IN_SIZES = (A_WIDTH, A_WIDTH, N_HEADS * HEAD_DIM, KV_WIDTH, KV_WIDTH,
            IDX_HEADS * IDX_DIM, IDX_DIM, IDX_HEADS, D_MODEL, D_MODEL)

LANES = 128
ROW_TILE = 512
QUERY_BLOCK = 256
KEY_BLOCK = 512
HEAD_STACK = 2
FF_CHUNK = 256
VMEM_LIMIT = 56 * 1024 * 1024
Q_SCALE = (HEAD_DIM ** -0.5) * 1.4426950408889634
INT_MIN = -2147483648
NEG_BIAS = -1e30


def _cparams(sem):
    return pltpu.CompilerParams(dimension_semantics=sem, vmem_limit_bytes=VMEM_LIMIT)


def _const_spec(shape):
    nd = len(shape)
    return pl.BlockSpec(shape, lambda *_: (0,) * nd, pipeline_mode=pl.Buffered(1))


def _gelu(x):
    return 0.5 * x * (1.0 + jnp.tanh(0.7978845608028654 * (x + 0.044715 * (x * x * x))))


def _rms(x, g):
    return x * lax.rsqrt(jnp.mean(x * x, axis=-1, keepdims=True) + RMS_EPS) * g


def _dot(a, b):
    return jnp.dot(a, b, preferred_element_type=F32)


def _dot_t(a, b):
    return lax.dot_general(a, b, (((1,), (1,)), ((), ())), preferred_element_type=F32)


def _float_key(x):
    b = pltpu.bitcast(x, I32)
    return jnp.where(b < 0, b ^ jnp.int32(0x7FFFFFFF), b)


def _memkv_kernel(x_ref, w_ref, kf_ref, vf_ref, kb_ref, vb_ref):
    z = _dot(x_ref[...].astype(BF16), w_ref[...])
    k = z[:, :MEM_WIDTH]
    v = z[:, MEM_WIDTH:]
    kf_ref[...] = k
    vf_ref[...] = v
    kb_ref[...] = k.astype(BF16)
    vb_ref[...] = v.astype(BF16)


def _memkv(mem2d, w_bf):
    m = mem2d.shape[0]
    tm = min(ROW_TILE, m)
    out = [jax.ShapeDtypeStruct((m, MEM_WIDTH), F32)] * 2 + [jax.ShapeDtypeStruct((m, MEM_WIDTH), BF16)] * 2
    row = lambda w: pl.BlockSpec((tm, w), lambda i: (i, 0))
    return pl.pallas_call(
        _memkv_kernel,
        grid=(m // tm,),
        in_specs=[row(D_MODEL), _const_spec((D_MODEL, 2 * MEM_WIDTH))],
        out_specs=[row(MEM_WIDTH)] * 4,
        out_shape=out,
        compiler_params=_cparams(("parallel",)),
        name="memkv",
    )(mem2d, w_bf)


def _rope_full(z, cos, sin_signed):
    return z * cos + pltpu.roll(z, HEAD_DIM // 2, axis=1) * sin_signed


def _rope_idx(z, cos, sin_signed, lane):
    partner = jnp.where((lane % IDX_DIM) < IDX_DIM // 2,
                        pltpu.roll(z, LANES - IDX_DIM // 2, axis=1),
                        pltpu.roll(z, IDX_DIM // 2, axis=1))
    return z * cos + partner * sin_signed


def _in_proj_kernel(chunk, x_ref, g_ref, wu, wv, wq, wk, wvv, wqi, wkw, wga, wgb, lng, lnb, wsp, bsp,
                    c128, s128, c64, s64,
                    a_ref, vrow_ref, q_ref, kf_ref, kb_ref, vf_ref, vb_ref, qi_ref, kw_ref, kwb_ref,
                    sga_ref, sgb_ref):
    tm = x_ref.shape[0]
    h = _rms(x_ref[...], g_ref[...]).astype(BF16)
    lane = lax.broadcasted_iota(I32, (tm, LANES), 1)

    gu = _gelu(_dot(h, wu[...]))
    gv = _gelu(_dot(h, wv[...]))
    mu = jnp.mean(gv, axis=-1, keepdims=True)
    vc = gv - mu
    var = jnp.mean(vc * vc, axis=-1, keepdims=True)
    vln = vc * lax.rsqrt(var + LN_EPS) * lng[...] + lnb[...]
    if chunk == 1:
        vrow_ref[...] = vln
        a_ref[...] = (gu * (vln * wsp[...] + bsp[...])).astype(BF16)
    else:
        vrow_ref[0] = vln[tm - chunk:, :]
        tri = (lax.broadcasted_iota(I32, (chunk, chunk), 0) >= lax.broadcasted_iota(I32, (chunk, chunk), 1))
        vlb = vln.astype(BF16)
        for g in range(A_GROUPS):
            wt = jnp.where(tri, wsp[g], 0.0).astype(BF16)
            cs = slice(g * A_GROUP_DIM, (g + 1) * A_GROUP_DIM)
            for c in range(tm // chunk):
                rs = slice(c * chunk, (c + 1) * chunk)
                sv = _dot(wt, vlb[rs, cs]) + bsp[:, cs]
                a_ref[rs, cs] = (gu[rs, cs] * sv).astype(BF16)

    cos = c128[...]
    sin = s128[...]
    zq = _dot(h, wq[...])
    for hh in range(N_HEADS):
        cs = slice(hh * HEAD_DIM, (hh + 1) * HEAD_DIM)
        q_ref[:, cs] = (_rope_full(zq[:, cs], cos, sin) * Q_SCALE).astype(BF16)
    zk = _dot(h, wk[...])
    for hh in range(N_KV):
        cs = slice(hh * HEAD_DIM, (hh + 1) * HEAD_DIM)
        kr = _rope_full(zk[:, cs], cos, sin)
        kf_ref[:, cs] = kr
        kb_ref[:, cs] = kr.astype(BF16)
    zv = _dot(h, wvv[...])
    vf_ref[...] = zv
    if chunk == 1:
        vb_ref[...] = zv.astype(BF16)
    else:
        vb_ref[0] = zv.T.astype(BF16)

    ci = c64[...]
    si = s64[...]
    zqi = _dot(h, wqi[...])
    low = lane < IDX_DIM
    for p in range(IDX_HEADS // 2):
        r = _rope_idx(zqi[:, p * LANES:(p + 1) * LANES], ci, si, lane)
        qi_ref[:, (2 * p) * LANES:(2 * p + 1) * LANES] = jnp.where(low, r, 0.0).astype(BF16)
        qi_ref[:, (2 * p + 1) * LANES:(2 * p + 2) * LANES] = jnp.where(
            low, pltpu.roll(r, IDX_DIM, axis=1), 0.0).astype(BF16)
    zkw = _dot(h, wkw[...])
    kir = _rope_idx(zkw, ci, si, lane)
    wscale = (IDX_HEADS ** -0.5) * (IDX_DIM ** -0.5)
    kw_ref[...] = jnp.where(low, kir, zkw * wscale)
    kwb_ref[...] = jnp.where(low, kir, 0.0).astype(BF16)

    sga_ref[...] = jax.nn.sigmoid(_dot(h, wga[...])).astype(BF16)
    sgb_ref[...] = jax.nn.sigmoid(_dot(h, wgb[...])).astype(BF16)


def _in_proj(x2d, lw, tabs, seq, chunk):
    m = x2d.shape[0]
    tm = min(ROW_TILE, seq) if chunk != 1 else m
    nt = seq // tm if chunk != 1 else 1
    nb = m // seq if chunk != 1 else 1
    row = lambda w: pl.BlockSpec((tm, w), lambda i: (i, 0))
    tab = pl.BlockSpec((tm, LANES), lambda i: (i % nt, 0))
    if chunk == 1:
        vrow_spec = row(A_WIDTH)
        vrow_shape = jax.ShapeDtypeStruct((m, A_WIDTH), F32)
        sp_specs = [_const_spec((1, A_WIDTH)), _const_spec((1, A_WIDTH))]
    else:
        vrow_spec = pl.BlockSpec((1, chunk, A_WIDTH), lambda i: (i // nt, 0, 0))
        vrow_shape = jax.ShapeDtypeStruct((nb, chunk, A_WIDTH), F32)
        sp_specs = [_const_spec((A_GROUPS, chunk, chunk)), _const_spec((chunk, A_WIDTH))]
    widths = [(A_WIDTH, BF16), None, (N_HEADS * HEAD_DIM, BF16), (KV_WIDTH, F32), (KV_WIDTH, BF16),
              (KV_WIDTH, F32), (KV_WIDTH, BF16), (IDX_HEADS * LANES, BF16), (LANES, F32), (LANES, BF16),
              (D_MODEL, BF16), (D_MODEL, BF16)]
    out_specs = [vrow_spec if w is None else row(w[0]) for w in widths]
    out_shape = [vrow_shape if w is None else jax.ShapeDtypeStruct((m, w[0]), w[1]) for w in widths]
    if chunk != 1:
        out_specs[6] = pl.BlockSpec((1, KV_WIDTH, tm), lambda i: (i // nt, 0, i % nt))
        out_shape[6] = jax.ShapeDtypeStruct((nb, KV_WIDTH, seq), BF16)
    wshapes = [(D_MODEL, A_WIDTH), (D_MODEL, A_WIDTH), (D_MODEL, N_HEADS * HEAD_DIM), (D_MODEL, KV_WIDTH),
               (D_MODEL, KV_WIDTH), (D_MODEL, IDX_HEADS * IDX_DIM), (D_MODEL, LANES), (D_MODEL, D_MODEL),
               (D_MODEL, D_MODEL)]
    in_specs = ([row(D_MODEL), _const_spec((1, D_MODEL))] + [_const_spec(s) for s in wshapes]
                + [_const_spec((1, A_WIDTH)), _const_spec((1, A_WIDTH))] + sp_specs + [tab] * 4)
    sp = (lw["sp_w1"], lw["sp_b1"]) if chunk == 1 else (lw["sp_w"], lw["sp_b"])
    return pl.pallas_call(
        functools.partial(_in_proj_kernel, chunk),
        grid=(m // tm,),
        in_specs=in_specs,
        out_specs=out_specs,
        out_shape=out_shape,
        compiler_params=_cparams(("arbitrary",)),
        name="in_proj",
    )(x2d, lw["g0"], lw["wu"], lw["wv"], lw["wq"], lw["wk"], lw["wvv"], lw["wqi"], lw["wkw"], lw["wga"],
      lw["wgb"], lw["ln_g"], lw["ln_b"], sp[0], sp[1], *tabs)


def _topk_threshold(count_ge, topk, shape):
    t0 = jnp.where(count_ge(jnp.zeros(shape, I32)) >= topk, jnp.int32(0), jnp.int32(INT_MIN))

    def bit_body(b, t):
        cand = t + lax.shift_left(jnp.int32(1), jnp.int32(30) - b)
        return jnp.where(count_ge(cand) >= topk, cand, t)

    return lax.fori_loop(0, 31, bit_body, jnp.broadcast_to(t0, shape))


def _group_sum(x):
    chains = 4
    accs = [x[c * 8:(c + 1) * 8, :] for c in range(chains)]
    for r in range(chains, x.shape[0] // 8):
        accs[r % chains] = accs[r % chains] + x[r * 8:(r + 1) * 8, :]
    return (accs[0] + accs[1]) + (accs[2] + accs[3])


def _dsa_prompt_t_kernel(topk, kb, q_ref, qi_ref, kw_ref, kib_ref, k_ref, vt_ref, o_ref, qis, keys, bias, qn, acc_ref):
    tq = q_ref.shape[1]
    seq = k_ref.shape[1]
    i = pl.program_id(1)
    nkb = ((i + 1) * tq + kb - 1) // kb
    qpos = i * tq + lax.broadcasted_iota(I32, (1, tq), 1)
    krow = lax.broadcasted_iota(I32, (kb, tq), 0)
    shape = (1, tq)

    for h in range(IDX_HEADS):
        qis[h * tq:(h + 1) * tq, :] = qi_ref[0, :, h * LANES:(h + 1) * LANES]
    kwt = kw_ref[0].T
    wi = [kwt[IDX_DIM + h:IDX_DIM + h + 1, :] for h in range(IDX_HEADS)]

    def score_body(j, carry):
        base = pl.multiple_of(j * kb, kb)
        lg = _dot_t(kib_ref[0, pl.ds(base, kb), :], qis[...])
        acc = jnp.zeros((kb, tq), F32)
        for h in range(IDX_HEADS):
            acc = acc + wi[h] * jnp.maximum(lg[:, h * tq:(h + 1) * tq], 0.0)
        keys[pl.ds(base, kb), :] = jnp.where(base + krow <= qpos, _float_key(acc), jnp.int32(INT_MIN))
        return carry

    lax.fori_loop(0, nkb, score_body, 0)

    sub = 64
    srow = lax.broadcasted_iota(I32, (sub, tq), 0)

    def count(pred):
        def body(j, part):
            base = pl.multiple_of(j * kb, kb)
            for r in range(kb // sub):
                lo = pl.multiple_of(base + r * sub, sub)
                hit = jnp.where(pred(keys[pl.ds(lo, sub), :], lo + srow), 1.0, 0.0)
                part = part + _group_sum(hit)
            return part
        part = lax.fori_loop(0, nkb, body, jnp.zeros((8, tq), F32))
        return jnp.sum(part, axis=0, keepdims=True)

    thr = _topk_threshold(lambda cand: count(lambda k, pos: k >= cand), float(topk), shape)
    thr = jnp.maximum(thr, jnp.int32(INT_MIN + 1))

    def write_bias(sel):
        def body(j, carry):
            base = pl.multiple_of(j * kb, kb)
            bias[pl.ds(base, kb), :] = jnp.where(sel(keys[pl.ds(base, kb), :], base + krow), 0.0, NEG_BIAS)
            return carry
        lax.fori_loop(0, nkb, body, 0)

    n_ge = count(lambda k, pos: k >= thr)
    has_ties = jnp.max(n_ge) > float(topk)

    @pl.when(jnp.logical_not(has_ties))
    def _():
        write_bias(lambda k, pos: k >= thr)

    @pl.when(has_ties)
    def _():
        need = float(topk) - count(lambda k, pos: k > thr)
        nbits = int(np.ceil(np.log2(seq))) + 1

        def bit_body(b, p):
            cand = p + lax.shift_left(jnp.int32(1), jnp.int32(nbits - 1) - b)
            below = count(lambda k, pos: jnp.logical_and(k == thr, pos < cand))
            return jnp.where(below < need, cand, p)

        last = lax.fori_loop(0, nbits, bit_body, jnp.zeros(shape, I32))
        write_bias(lambda k, pos: jnp.logical_or(k > thr, jnp.logical_and(k == thr, pos <= last)))

    ngrp = qn.shape[0]
    stack = N_HEADS // ngrp
    for u in range(ngrp):
        for g in range(stack):
            hh = stack * u + g
            qn[u, g * tq:(g + 1) * tq, :] = q_ref[0, :, hh * HEAD_DIM:(hh + 1) * HEAD_DIM]
    acc_ref[...] = jnp.zeros(acc_ref.shape, F32)

    def att_body(j, carry):
        ms, ls = carry
        base = pl.multiple_of(j * kb, kb)
        bb = bias[pl.ds(base, kb), :]
        if stack > 1:
            bb = jnp.concatenate([bb] * stack, axis=1)
        new_m, new_l = [], []
        for u in range(ngrp):
            n = (u * stack) // GQA
            cs = slice(n * HEAD_DIM, (n + 1) * HEAD_DIM)
            s = _dot_t(k_ref[0, pl.ds(base, kb), cs], qn[u]) + bb
            m_new = jnp.maximum(ms[u], jnp.max(s, axis=0, keepdims=True))
            p = jnp.exp2(s - m_new)
            alpha = jnp.exp2(ms[u] - m_new)
            new_l.append(alpha * ls[u] + jnp.sum(p, axis=0, keepdims=True))
            acc_ref[u] = alpha * acc_ref[u] + _dot(vt_ref[0, cs, pl.ds(base, kb)], p.astype(BF16))
            new_m.append(m_new)
        return tuple(new_m), tuple(new_l)

    m0 = tuple(jnp.full((1, stack * tq), -jnp.inf, F32) for _ in range(ngrp))
    l0 = tuple(jnp.zeros((1, stack * tq), F32) for _ in range(ngrp))
    _, ls = lax.fori_loop(0, nkb, att_body, (m0, l0))
    for u in range(ngrp):
        o = acc_ref[u] / ls[u]
        for g in range(stack):
            hh = stack * u + g
            o_ref[0, :, hh * HEAD_DIM:(hh + 1) * HEAD_DIM] = o[:, g * tq:(g + 1) * tq].T.astype(BF16)


def _dsa_prompt_t(q, qi, kw, kib, kb16, vt, nb, seq):
    tq = min(QUERY_BLOCK, seq)
    kb = min(KEY_BLOCK, seq)
    topk = min(TOPK_MAX, seq // 4)
    r3 = lambda a: a.reshape(nb, seq, a.shape[-1])
    qblk = lambda w: pl.BlockSpec((1, tq, w), lambda b, i: (b, i, 0))
    full = lambda w: pl.BlockSpec((1, seq, w), lambda b, i: (b, 0, 0))
    out = pl.pallas_call(
        functools.partial(_dsa_prompt_t_kernel, topk, kb),
        grid=(nb, seq // tq),
        in_specs=[qblk(N_HEADS * HEAD_DIM), qblk(IDX_HEADS * LANES), qblk(LANES), full(LANES), full(KV_WIDTH),
                  pl.BlockSpec((1, KV_WIDTH, seq), lambda b, i: (b, 0, 0))],
        out_specs=qblk(N_HEADS * HEAD_DIM),
        out_shape=jax.ShapeDtypeStruct((nb, seq, N_HEADS * HEAD_DIM), BF16),
        scratch_shapes=[pltpu.VMEM((IDX_HEADS * tq, LANES), BF16), pltpu.VMEM((seq, tq), I32),
                        pltpu.VMEM((seq, tq), F32),
                        pltpu.VMEM((N_HEADS // HEAD_STACK, HEAD_STACK * tq, HEAD_DIM), BF16),
                        pltpu.VMEM((N_HEADS // HEAD_STACK, HEAD_DIM, HEAD_STACK * tq), F32)],
        compiler_params=_cparams(("parallel", "arbitrary")),
        name="dsa_prompt",
    )(r3(q), r3(qi), r3(kw), r3(kib), r3(kb16), vt)
    return out.reshape(nb * seq, N_HEADS * HEAD_DIM)


def _dsa_select_kernel(layer, topk, npages, pt_ref, qi_ref, kw_ref, kwb_ref, cik_ref, idx_ref, ibuf, keys, grank, isem):
    b = pl.program_id(0)
    nrow = keys.shape[0]
    group = 8

    def idx_copy(j):
        return pltpu.make_async_copy(cik_ref.at[layer, pt_ref[b, j]], ibuf.at[j], isem)

    def start_idx(j, c):
        idx_copy(j).start()
        return c

    lax.fori_loop(0, npages, start_idx, 0)

    qif = qi_ref[0].astype(F32)
    qi8f = jnp.concatenate([qif[:, h * LANES:h * LANES + IDX_DIM] for h in range(IDX_HEADS)], axis=0)
    qi8 = qi8f.astype(BF16)
    kwrow = kw_ref[0]
    eye = (lax.broadcasted_iota(I32, (IDX_HEADS, LANES), 0) + IDX_DIM
           == lax.broadcasted_iota(I32, (IDX_HEADS, LANES), 1))
    wcol = jnp.sum(jnp.where(eye, jnp.broadcast_to(kwrow, (IDX_HEADS, LANES)), 0.0), axis=1, keepdims=True)

    def wait_idx(j, c):
        idx_copy(j).wait()
        return c

    lax.fori_loop(0, npages, wait_idx, 0)

    def score_body(jj, c):
        rows = []
        for u in range(group):
            lg = _dot(qi8, ibuf[jj * group + u].astype(BF16))
            rows.append(jnp.sum(wcol * jnp.maximum(lg, 0.0), axis=0, keepdims=True))
        tile = jnp.concatenate(rows, axis=0)
        keys[pl.ds(pl.multiple_of(jj * group, group), group), :] = _float_key(tile + 0.0)
        return c

    lax.fori_loop(0, npages // group, score_body, 0)
    lgs = jnp.sum(qi8f * kwb_ref[0].astype(F32)[:, :IDX_DIM], axis=1, keepdims=True)
    scs = jnp.sum(wcol * jnp.maximum(lgs, 0.0), axis=0, keepdims=True)
    tshape = (nrow - npages, LANES)
    first = jnp.logical_and(lax.broadcasted_iota(I32, tshape, 0) == 0, lax.broadcasted_iota(I32, tshape, 1) == 0)
    keys[npages:nrow, :] = jnp.where(first, _float_key(jnp.broadcast_to(scs, tshape) + 0.0), jnp.int32(INT_MIN))

    kall = keys[...]
    pos = lax.broadcasted_iota(I32, (nrow, LANES), 0) * LANES + lax.broadcasted_iota(I32, (nrow, LANES), 1)
    shape = (1, 1)

    def count(pred):
        hit = jnp.where(pred, 1.0, 0.0)
        return jnp.sum(jnp.sum(hit, axis=1, keepdims=True), axis=0, keepdims=True)

    thr = _topk_threshold(lambda cand: count(kall >= cand), float(topk), shape)
    thr = jnp.maximum(thr, jnp.int32(INT_MIN + 1))
    need = float(topk) - count(kall > thr)
    tied = kall == thr
    nbits = int(np.ceil(np.log2(nrow * LANES))) + 1

    def bit_body(bb, p):
        cand = p + lax.shift_left(jnp.int32(1), jnp.int32(nbits - 1) - bb)
        below = count(jnp.logical_and(tied, pos < cand))
        return jnp.where(below < need, cand, p)

    last = lax.fori_loop(0, nbits, bit_body, jnp.zeros(shape, I32))
    sel = jnp.logical_or(kall > thr, jnp.logical_and(tied, pos <= last))
    msk = jnp.where(sel, 1.0, 0.0)

    upper = (lax.broadcasted_iota(I32, (LANES, LANES), 0) <= lax.broadcasted_iota(I32, (LANES, LANES), 1))
    within = _dot(msk.astype(BF16), jnp.where(upper, 1.0, 0.0).astype(BF16))
    rowsum = jnp.broadcast_to(jnp.sum(msk, axis=1, keepdims=True), (nrow, LANES))
    before = (lax.broadcasted_iota(I32, (nrow, nrow), 1) < lax.broadcasted_iota(I32, (nrow, nrow), 0))
    offset = _dot(jnp.where(before, 1.0, 0.0).astype(BF16), rowsum.astype(BF16))
    grank[...] = jnp.where(sel, offset + within - 1.0, -1.0)

    slot = lax.broadcasted_iota(I32, (topk, LANES), 0).astype(F32)
    lane_f = lax.broadcasted_iota(I32, (1, LANES), 1).astype(F32)

    def compact_body(r, acc):
        g = jnp.broadcast_to(grank[pl.ds(r, 1), :], (topk, LANES))
        flat = jnp.broadcast_to(lax.convert_element_type(r * LANES, F32) + lane_f, (topk, LANES))
        return acc + jnp.where(g == slot, flat, 0.0)

    acc = lax.fori_loop(0, npages + 1, compact_body, jnp.zeros((topk, LANES), F32))
    idx_ref[0] = jnp.sum(acc, axis=1, keepdims=True).astype(I32)


def _dsa_attend_kernel(layer, topk, npages, pt_ref, ix_ref, q_ref, kn_ref, vn_ref, ck_ref, cv_ref, o_ref,
                       kg, vg, ksem, vsem):
    b = pl.program_id(0)
    past = npages * PAGE

    def issue(t, c):
        ix = ix_ref[b, t]

        @pl.when(ix < past)
        def _():
            pg = pt_ref[b, lax.shift_right_logical(ix, 7)]
            off = jnp.bitwise_and(ix, PAGE - 1)
            pltpu.make_async_copy(ck_ref.at[layer, pg, off], kg.at[t], ksem).start()
            pltpu.make_async_copy(cv_ref.at[layer, pg, off], vg.at[t], vsem).start()

        @pl.when(ix >= past)
        def _():
            pltpu.make_async_copy(kn_ref.at[b], kg.at[t], ksem).start()
            pltpu.make_async_copy(vn_ref.at[b], vg.at[t], vsem).start()

        return c

    lax.fori_loop(0, topk, issue, 0)

    def wait(t, c):
        pltpu.make_async_copy(kn_ref.at[b], kg.at[t], ksem).wait()
        pltpu.make_async_copy(vn_ref.at[b], vg.at[t], vsem).wait()
        return c

    lax.fori_loop(0, topk, wait, 0)

    qf = q_ref[0].astype(F32)
    pad = jnp.zeros((8 - GQA, HEAD_DIM), F32)
    for n in range(N_KV):
        rows = [qf[:, (GQA * n + g) * HEAD_DIM:(GQA * n + g + 1) * HEAD_DIM] for g in range(GQA)]
        q8 = jnp.concatenate(rows + [pad], axis=0).astype(BF16)
        s = _dot_t(q8, kg[:, n, :].astype(BF16))
        p = jnp.exp2(s - jnp.max(s, axis=1, keepdims=True))
        p = p / jnp.sum(p, axis=1, keepdims=True)
        o = _dot(p.astype(BF16), vg[:, n, :].astype(BF16))
        for g in range(GQA):
            hh = GQA * n + g
            o_ref[0, :, hh * HEAD_DIM:(hh + 1) * HEAD_DIM] = o[g:g + 1, :].astype(BF16)


def _dsa_sample(layer, q, qi, kw, kwb, kf, vf, cache_idx_k, cache_k, cache_v, page_table):
    nb, npages = page_table.shape
    assert npages % 8 == 0 and PAGE == 128
    topk = min(TOPK_MAX, (npages * PAGE + 1) // 4)
    nrow = -(-(npages + 1) // LANES) * LANES
    r3 = lambda a: a.reshape(nb, 1, a.shape[-1])
    anyspec = pl.BlockSpec(memory_space=pl.ANY)
    blk1 = lambda w: pl.BlockSpec((1, 1, w), lambda b, pt: (b, 0, 0))
    idx = pl.pallas_call(
        functools.partial(_dsa_select_kernel, layer, topk, npages),
        grid_spec=pltpu.PrefetchScalarGridSpec(
            num_scalar_prefetch=1,
            grid=(nb,),
            in_specs=[blk1(IDX_HEADS * LANES), blk1(LANES), blk1(LANES), anyspec],
            out_specs=pl.BlockSpec((1, topk, 1), lambda b, pt: (b, 0, 0)),
            scratch_shapes=[pltpu.VMEM((npages, IDX_DIM, PAGE), F32), pltpu.VMEM((nrow, LANES), I32),
                            pltpu.VMEM((nrow, LANES), F32), pltpu.SemaphoreType.DMA(())],
        ),
        out_shape=jax.ShapeDtypeStruct((nb, topk, 1), I32),
        compiler_params=_cparams(("arbitrary",)),
        name="dsa_select",
    )(page_table, r3(qi), r3(kw), r3(kwb), jnp.swapaxes(cache_idx_k, 2, 3))
    blk2 = lambda w: pl.BlockSpec((1, 1, w), lambda b, pt, ix: (b, 0, 0))
    rowshape = (topk, N_KV, HEAD_DIM)
    out = pl.pallas_call(
        functools.partial(_dsa_attend_kernel, layer, topk, npages),
        grid_spec=pltpu.PrefetchScalarGridSpec(
            num_scalar_prefetch=2,
            grid=(nb,),
            in_specs=[blk2(N_HEADS * HEAD_DIM), anyspec, anyspec, anyspec, anyspec],
            out_specs=blk2(N_HEADS * HEAD_DIM),
            scratch_shapes=[pltpu.VMEM(rowshape, F32), pltpu.VMEM(rowshape, F32),
                            pltpu.SemaphoreType.DMA(()), pltpu.SemaphoreType.DMA(())],
        ),
        out_shape=jax.ShapeDtypeStruct((nb, 1, N_HEADS * HEAD_DIM), BF16),
        compiler_params=_cparams(("arbitrary",)),
        name="dsa_attend",
    )(page_table, idx.reshape(nb, topk), r3(q), kf.reshape(nb, N_KV, HEAD_DIM), vf.reshape(nb, N_KV, HEAD_DIM),
      cache_k, cache_v)
    return out.reshape(nb, N_HEADS * HEAD_DIM)


def _mix_out_kernel(a_ref, o_ref, sga_ref, sgb_ref, x_ref, wba, wbb, wout, g_ref, y_ref):
    merged = (sga_ref[...].astype(F32) * _dot(a_ref[...], wba[...])
              + sgb_ref[...].astype(F32) * _dot(o_ref[...], wbb[...]))
    mix = _dot(merged.astype(BF16), wout[...])
    y_ref[...] = x_ref[...] + _rms(mix, g_ref[...])


def _mix_out(a, o, sga, sgb, x2d, lw):
    m = x2d.shape[0]
    tm = min(ROW_TILE, m)
    row = lambda w: pl.BlockSpec((tm, w), lambda i: (i, 0))
    return pl.pallas_call(
        _mix_out_kernel,
        grid=(m // tm,),
        in_specs=[row(A_WIDTH), row(D_MODEL), row(D_MODEL), row(D_MODEL), row(D_MODEL),
                  _const_spec((A_WIDTH, D_MODEL)), _const_spec((D_MODEL, D_MODEL)), _const_spec((D_MODEL, D_MODEL)),
                  _const_spec((1, D_MODEL))],
        out_specs=row(D_MODEL),
        out_shape=jax.ShapeDtypeStruct((m, D_MODEL), F32),
        compiler_params=_cparams(("parallel",)),
        name="mix_out",
    )(a, o, sga, sgb, x2d, lw["wba"], lw["wbb"], lw["wout"], lw["g1"])


def _cross_prompt_kernel(x_ref, g2, g3, wq, wo, mk_ref, mv_ref, y_ref):
    x = x_ref[0]
    h = _rms(x, g2[...]).astype(BF16)
    qm = (_dot(h, wq[...]) * (MEM_HEAD_DIM ** -0.5)).astype(BF16)
    outs = []
    for hh in range(MEM_HEADS):
        cs = slice(hh * MEM_HEAD_DIM, (hh + 1) * MEM_HEAD_DIM)
        s = _dot_t(qm[:, cs], mk_ref[0, :, cs])
        p = jnp.exp(s - jnp.max(s, axis=1, keepdims=True))
        p = p / jnp.sum(p, axis=1, keepdims=True)
        outs.append(_dot(p.astype(BF16), mv_ref[0, :, cs]).astype(BF16))
    c = _dot(jnp.concatenate(outs, axis=1), wo[...])
    y_ref[0] = x + _rms(c, g3[...])


def _cross_prompt(x2d, mk, mv, lw, nb, seq):
    tm = min(ROW_TILE, seq)
    mt = mk.shape[0] // nb
    xb = pl.BlockSpec((1, tm, D_MODEL), lambda b, i: (b, i, 0))
    mb = pl.BlockSpec((1, mt, MEM_WIDTH), lambda b, i: (b, 0, 0))
    out = pl.pallas_call(
        _cross_prompt_kernel,
        grid=(nb, seq // tm),
        in_specs=[xb, _const_spec((1, D_MODEL)), _const_spec((1, D_MODEL)), _const_spec((D_MODEL, MEM_WIDTH)),
                  _const_spec((MEM_WIDTH, D_MODEL)), mb, mb],
        out_specs=xb,
        out_shape=jax.ShapeDtypeStruct((nb, seq, D_MODEL), F32),
        compiler_params=_cparams(("parallel", "parallel")),
        name="cross_prompt",
    )(x2d.reshape(nb, seq, D_MODEL), lw["g2"], lw["g3"], lw["wmq"], lw["wmo"],
      mk.reshape(nb, mt, MEM_WIDTH), mv.reshape(nb, mt, MEM_WIDTH))
    return out.reshape(nb * seq, D_MODEL)


def _cross_sample_kernel(x_ref, g2, g3, wq, wo, mk_ref, mv_ref, y_ref):
    x = x_ref[...]
    bt = x.shape[0]
    h = _rms(x, g2[...]).astype(BF16)
    qm = (_dot(h, wq[...]) * (MEM_HEAD_DIM ** -0.5)).astype(BF16)
    rows = []
    for b in range(bt):
        qb = jnp.broadcast_to(qm[b:b + 1, :], (8, MEM_WIDTH))
        outs = []
        for hh in range(MEM_HEADS):
            cs = slice(hh * MEM_HEAD_DIM, (hh + 1) * MEM_HEAD_DIM)
            s = _dot_t(qb[:, cs], mk_ref[0, b, :, hh, :].astype(BF16))
            p = jnp.exp(s - jnp.max(s, axis=1, keepdims=True))
            p = p / jnp.sum(p, axis=1, keepdims=True)
            outs.append(_dot(p.astype(BF16), mv_ref[0, b, :, hh, :].astype(BF16)))
        rows.append(jnp.concatenate(outs, axis=1)[0:1, :])
    att = jnp.concatenate(rows, axis=0).astype(BF16)
    c = _dot(att, wo[...])
    y_ref[...] = x + _rms(c, g3[...])


def _cross_sample(layer, x2d, mk, mv, lw):
    nb = x2d.shape[0]
    bt = 8
    mt = mk.shape[2]
    xb = pl.BlockSpec((bt, D_MODEL), lambda i: (i, 0))
    mb = pl.BlockSpec((1, bt, mt, MEM_HEADS, MEM_HEAD_DIM), lambda i: (layer, i, 0, 0, 0))
    return pl.pallas_call(
        _cross_sample_kernel,
        grid=(nb // bt,),
        in_specs=[xb, _const_spec((1, D_MODEL)), _const_spec((1, D_MODEL)), _const_spec((D_MODEL, MEM_WIDTH)),
                  _const_spec((MEM_WIDTH, D_MODEL)), mb, mb],
        out_specs=xb,
        out_shape=jax.ShapeDtypeStruct((nb, D_MODEL), F32),
        compiler_params=_cparams(("parallel",)),
        name="cross_sample",
    )(x2d, lw["g2"], lw["g3"], lw["wmq"], lw["wmo"], mk, mv)


def _ffn_seq_kernel(x_ref, g4, g5, wup, cw, cb, wdn, st_ref, y_ref, ns_ref, carry, ext, acc):
    tm = x_ref.shape[1]
    t = pl.program_id(1)

    @pl.when(t == 0)
    def _():
        carry[6:8, :] = st_ref[0]

    x = x_ref[0]
    h = _rms(x, g4[...]).astype(BF16)
    for c in range(D_FF // FF_CHUNK):
        prod = None
        for half in range(2):
            cs = slice(half * D_FF + c * FF_CHUNK, half * D_FF + (c + 1) * FF_CHUNK)
            up = _dot(h, wup[:, cs])
            ext[half, 0:8, :] = carry[:, cs]
            ext[half, 8:8 + tm, :] = up
            carry[:, cs] = up[tm - 8:, :]
            conv = (cb[:, cs] + cw[0:1, cs] * ext[half, 6:6 + tm, :] + cw[1:2, cs] * ext[half, 7:7 + tm, :]
                    + cw[2:3, cs] * up)
            prod = _gelu(conv) if half == 0 else prod * conv
        part = _dot(prod.astype(BF16), wdn[c * FF_CHUNK:(c + 1) * FF_CHUNK, :])
        if c == 0:
            acc[...] = part
        else:
            acc[...] += part
    y_ref[0] = x + _rms(acc[...], g5[...])
    ns_ref[0] = carry[6:8, :]


def _ffn_seq(x2d, state, lw, nb, seq):
    tm = min(ROW_TILE, seq)
    xb = pl.BlockSpec((1, tm, D_MODEL), lambda b, t: (b, t, 0))
    sb = pl.BlockSpec((1, CONV_WIDTH - 1, 2 * D_FF), lambda b, t: (b, 0, 0))
    y, ns = pl.pallas_call(
        _ffn_seq_kernel,
        grid=(nb, seq // tm),
        in_specs=[xb, _const_spec((1, D_MODEL)), _const_spec((1, D_MODEL)), _const_spec((D_MODEL, 2 * D_FF)),
                  _const_spec((CONV_WIDTH, 2 * D_FF)), _const_spec((1, 2 * D_FF)), _const_spec((D_FF, D_MODEL)), sb],
        out_specs=[xb, sb],
        out_shape=[jax.ShapeDtypeStruct((nb, seq, D_MODEL), F32),
                   jax.ShapeDtypeStruct((nb, CONV_WIDTH - 1, 2 * D_FF), F32)],
        scratch_shapes=[pltpu.VMEM((8, 2 * D_FF), F32), pltpu.VMEM((2, tm + 8, FF_CHUNK), F32),
                        pltpu.VMEM((tm, D_MODEL), F32)],
        compiler_params=_cparams(("arbitrary", "arbitrary")),
        name="ffn_seq",
    )(x2d.reshape(nb, seq, D_MODEL), lw["g4"], lw["g5"], lw["wup"], lw["cw"], lw["cb"], lw["wdn"], state)
    return y.reshape(nb * seq, D_MODEL), ns


def _ffn_step_kernel(x_ref, g4, g5, wup, cw, cb, wdn, s0_ref, s1_ref, y_ref, up_ref):
    x = x_ref[...]
    h = _rms(x, g4[...]).astype(BF16)
    acc = None
    for c in range(D_FF // FF_CHUNK):
        prod = None
        for half in range(2):
            cs = slice(half * D_FF + c * FF_CHUNK, half * D_FF + (c + 1) * FF_CHUNK)
            up = _dot(h, wup[:, cs])
            up_ref[:, cs] = up
            conv = cb[:, cs] + cw[0:1, cs] * s0_ref[:, cs] + cw[1:2, cs] * s1_ref[:, cs] + cw[2:3, cs] * up
            prod = _gelu(conv) if half == 0 else prod * conv
        part = _dot(prod.astype(BF16), wdn[c * FF_CHUNK:(c + 1) * FF_CHUNK, :])
        acc = part if acc is None else acc + part
    y_ref[...] = x + _rms(acc, g5[...])


def _ffn_step(x2d, state, lw):
    nb = x2d.shape[0]
    s0 = state[:, 0, :]
    s1 = state[:, 1, :]
    full = lambda w: pl.BlockSpec((nb, w), lambda i: (0, 0))
    y, up = pl.pallas_call(
        _ffn_step_kernel,
        grid=(1,),
        in_specs=[full(D_MODEL), _const_spec((1, D_MODEL)), _const_spec((1, D_MODEL)),
                  _const_spec((D_MODEL, 2 * D_FF)), _const_spec((CONV_WIDTH, 2 * D_FF)), _const_spec((1, 2 * D_FF)),
                  _const_spec((D_FF, D_MODEL)), full(2 * D_FF), full(2 * D_FF)],
        out_specs=[full(D_MODEL), full(2 * D_FF)],
        out_shape=[jax.ShapeDtypeStruct((nb, D_MODEL), F32), jax.ShapeDtypeStruct((nb, 2 * D_FF), F32)],
        compiler_params=_cparams(("arbitrary",)),
        name="ffn_step",
    )(x2d, lw["g4"], lw["g5"], lw["wup"], lw["cw"], lw["cb"], lw["wdn"], s0, s1)
    return y, jnp.stack([s1, up], axis=1)


def _rope_tables(pos):
    def tab(dim, reps):
        half = dim // 2
        inv = ROPE_THETA ** (-jnp.arange(half, dtype=F32) / half)
        ang = pos.astype(F32)[:, None] * inv[None, :]
        cos = jnp.concatenate([jnp.cos(ang)] * 2, axis=1)
        sin = jnp.concatenate([-jnp.sin(ang), jnp.sin(ang)], axis=1)
        return jnp.tile(cos, (1, reps)), jnp.tile(sin, (1, reps))
    c128, s128 = tab(HEAD_DIM, 1)
    c64, s64 = tab(IDX_DIM, LANES // IDX_DIM)
    return c128, s128, c64, s64


def _layer_weights(l, norms, w_in, ln_v_g, ln_v_b, w_spatial, b_spatial, w_branch_a, w_branch_b, w_out, w_mem_q,
                   w_mem_kv, w_mem_o, w_up, conv_w, conv_b, w_down):
    offs = np.cumsum((0,) + IN_SIZES)
    seg = lambda i: w_in[l][:, offs[i]:offs[i + 1]].astype(BF16)
    wkw = jnp.concatenate([w_in[l][:, offs[6]:offs[8]],
                           jnp.zeros((D_MODEL, LANES - IDX_DIM - IDX_HEADS), F32)], axis=1).astype(BF16)
    lw = dict(
        wu=seg(0), wv=seg(1), wq=seg(2), wk=seg(3), wvv=seg(4), wqi=seg(5), wkw=wkw, wga=seg(8), wgb=seg(9),
        ln_g=ln_v_g[l][None, :], ln_b=ln_v_b[l][None, :],
        sp_w=w_spatial[l], sp_b=jnp.repeat(b_spatial[l].T, A_GROUP_DIM, axis=1),
        sp_w1=jnp.repeat(w_spatial[l][:, 0, 0], A_GROUP_DIM)[None, :],
        sp_b1=jnp.repeat(b_spatial[l][:, 0], A_GROUP_DIM)[None, :],
        wba=w_branch_a[l].astype(BF16), wbb=w_branch_b[l].astype(BF16), wout=w_out[l].astype(BF16),
        wmq=w_mem_q[l].astype(BF16), wmkv=w_mem_kv[l].astype(BF16), wmo=w_mem_o[l].astype(BF16),
        wup=w_up[l].astype(BF16), cw=conv_w[l], cb=conv_b[l][None, :], wdn=w_down[l].astype(BF16),
    )
    for i in range(6):
        lw["g%d" % i] = norms[l, i][None, :]
    return lw


def kernel(x_prompt, x_sample, cache_k, cache_v, cache_idx_k, cache_mem_k, cache_mem_v, state_conv, page_table,
           mem_prompt, norms, w_in, ln_v_g, ln_v_b, w_spatial, b_spatial, w_branch_a, w_branch_b, w_out,
           w_mem_q, w_mem_kv, w_mem_o, w_up, conv_w, conv_b, w_down):
    nb, seq, _ = x_prompt.shape
    nd, tdec, _ = x_sample.shape
    assert tdec == 1 and seq % CHUNK == 0
    depth = norms.shape[0]
    mt = mem_prompt.shape[1]
    past = page_table.shape[1] * PAGE

    tabs_p = _rope_tables(jnp.arange(seq))
    tabs_s = tuple(jnp.broadcast_to(t, (nd, LANES)) for t in _rope_tables(jnp.full((1,), past)))
    yp = x_prompt.reshape(nb * seq, D_MODEL)
    ys = x_sample.reshape(nd, D_MODEL)
    mem2d = mem_prompt.reshape(nb * mt, D_MODEL)
    zero_state = jnp.zeros((nb, CONV_WIDTH - 1, 2 * D_FF), F32)

    outs = [[] for _ in range(12)]
    for l in range(depth):
        lw = _layer_weights(l, norms, w_in, ln_v_g, ln_v_b, w_spatial, b_spatial, w_branch_a, w_branch_b, w_out,
                            w_mem_q, w_mem_kv, w_mem_o, w_up, conv_w, conv_b, w_down)
        mkf, mvf, mkb, mvb = _memkv(mem2d, lw["wmkv"])
        a, vrow, q, kf, kb16, vf, vt16, qi, kw, kwb, sga, sgb = _in_proj(yp, lw, tabs_p, seq, CHUNK)
        o = _dsa_prompt_t(q, qi, kw, kwb, kb16, vt16, nb, seq)
        yp = _mix_out(a, o, sga, sgb, yp, lw)
        yp = _cross_prompt(yp, mkb, mvb, lw, nb, seq)
        yp, cp = _ffn_seq(yp, zero_state, lw, nb, seq)
        outs[0].append(kf.reshape(nb, seq, N_KV, HEAD_DIM))
        outs[1].append(vf.reshape(nb, seq, N_KV, HEAD_DIM))
        outs[2].append(kw[:, :IDX_DIM].reshape(nb, seq, IDX_DIM))
        outs[6].append(mkf.reshape(nb, mt, MEM_HEADS, MEM_HEAD_DIM))
        outs[7].append(mvf.reshape(nb, mt, MEM_HEADS, MEM_HEAD_DIM))
        outs[8].append(vrow)
        outs[10].append(cp)
        a, vrow, q, kf, kb16, vf, vb16, qi, kw, kwb, sga, sgb = _in_proj(ys, lw, tabs_s, 1, 1)
        o = _dsa_sample(l, q, qi, kw, kwb, kf, vf, cache_idx_k, cache_k, cache_v, page_table)
        ys = _mix_out(a, o, sga, sgb, ys, lw)
        ys = _cross_sample(l, ys, cache_mem_k, cache_mem_v, lw)
        ys, cs = _ffn_step(ys, state_conv[l], lw)
        outs[3].append(kf.reshape(nd, 1, N_KV, HEAD_DIM))
        outs[4].append(vf.reshape(nd, 1, N_KV, HEAD_DIM))
        outs[5].append(kw[:, :IDX_DIM].reshape(nd, 1, IDX_DIM))
        outs[9].append(vrow.reshape(nd, 1, A_WIDTH))
        outs[11].append(cs)
    return (yp.reshape(nb, seq, D_MODEL), ys.reshape(nd, 1, D_MODEL)) + tuple(jnp.stack(o) for o in outs)
```

```python
import functools

import numpy as np
import jax
import jax.numpy as jnp
from jax import lax
from jax.experimental import pallas as pl
from jax.experimental.pallas import tpu as pltpu

F32 = jnp.float32
BF16 = jnp.bfloat16
I32 = jnp.int32

D_MODEL = 1024
PAGE = 128
CHUNK = 128
A_GROUPS = 4
A_GROUP_DIM = 128
A_WIDTH = A_GROUPS * A_GROUP_DIM
N_HEADS = 8
N_KV = 4
HEAD_DIM = 128
GQA = N_HEADS // N_KV
KV_WIDTH = N_KV * HEAD_DIM
IDX_HEADS = 8
IDX_DIM = 64
TOPK_MAX = 256
ROPE_THETA = 10000.0
MEM_HEADS = 4
MEM_HEAD_DIM = 128
MEM_WIDTH = MEM_HEADS * MEM_HEAD_DIM
D_FF = 2816
CONV_WIDTH = 3
RMS_EPS = 1e-6
LN_EPS = 1e-5
IN_SIZES = (A_WIDTH, A_WIDTH, N_HEADS * HEAD_DIM, KV_WIDTH, KV_WIDTH,
            IDX_HEADS * IDX_DIM, IDX_DIM, IDX_HEADS, D_MODEL, D_MODEL)

LANES = 128
ROW_TILE = 512
QUERY_BLOCK = 256
KEY_BLOCK = 512
HEAD_STACK = 2
VT_ROWS = HEAD_DIM + 16
FF_CHUNK = 2816
VMEM_LIMIT = 56 * 1024 * 1024
Q_SCALE = (HEAD_DIM ** -0.5) * 1.4426950408889634
INT_MIN = -2147483648
NEG_BIAS = -1e30


def _cparams(sem):
    return pltpu.CompilerParams(dimension_semantics=sem, vmem_limit_bytes=VMEM_LIMIT)


def _const_spec(shape):
    nd = len(shape)
    return pl.BlockSpec(shape, lambda *_: (0,) * nd, pipeline_mode=pl.Buffered(1))


def _gelu(x):
    return 0.5 * x * (1.0 + jnp.tanh(0.7978845608028654 * (x + 0.044715 * (x * x * x))))


def _rms(x, g):
    return x * lax.rsqrt(jnp.mean(x * x, axis=-1, keepdims=True) + RMS_EPS) * g


def _dot(a, b):
    return jnp.dot(a, b, preferred_element_type=F32)


def _dot_t(a, b):
    return lax.dot_general(a, b, (((1,), (1,)), ((), ())), preferred_element_type=F32)


def _float_key(x):
    b = pltpu.bitcast(x, I32)
    return jnp.where(b < 0, b ^ jnp.int32(0x7FFFFFFF), b)


def _memkv_kernel(x_ref, w_ref, kf_ref, vf_ref, kb_ref, vb_ref):
    z = _dot(x_ref[...].astype(BF16), w_ref[...])
    k = z[:, :MEM_WIDTH]
    v = z[:, MEM_WIDTH:]
    kf_ref[...] = k
    vf_ref[...] = v
    kb_ref[...] = k.astype(BF16)
    vb_ref[...] = v.astype(BF16)


def _memkv(mem2d, w_bf):
    m = mem2d.shape[0]
    tm = min(ROW_TILE, m)
    out = [jax.ShapeDtypeStruct((m, MEM_WIDTH), F32)] * 2 + [jax.ShapeDtypeStruct((m, MEM_WIDTH), BF16)] * 2
    row = lambda w: pl.BlockSpec((tm, w), lambda i: (i, 0))
    return pl.pallas_call(
        _memkv_kernel,
        grid=(m // tm,),
        in_specs=[row(D_MODEL), _const_spec((D_MODEL, 2 * MEM_WIDTH))],
        out_specs=[row(MEM_WIDTH)] * 4,
        out_shape=out,
        compiler_params=_cparams(("parallel",)),
        name="memkv",
    )(mem2d, w_bf)


def _rope_full(z, cos, sin_signed):
    return z * cos + pltpu.roll(z, HEAD_DIM // 2, axis=1) * sin_signed


def _rope_idx(z, cos, sin_signed, lane):
    partner = jnp.where((lane % IDX_DIM) < IDX_DIM // 2,
                        pltpu.roll(z, LANES - IDX_DIM // 2, axis=1),
                        pltpu.roll(z, IDX_DIM // 2, axis=1))
    return z * cos + partner * sin_signed


def _in_proj_kernel(chunk, layer, x_ref, g_ref, wu, wv, wq, wk, wvv, wqi, wkw, wga, wgb, lng, lnb, wsp, bsp,
                    c128, s128, c64, s64, kprev_ref, vprev_ref,
                    a_ref, vrow_ref, q_ref, kf_ref, kb_ref, vf_ref, vb_ref, qi_ref, kw_ref, kwb_ref,
                    sga_ref, sgb_ref):
    if layer > 0:
        kf_ref[0:layer] = kprev_ref[...]
        vf_ref[0:layer] = vprev_ref[...]
    tm = x_ref.shape[0]
    h = _rms(x_ref[...], g_ref[...]).astype(BF16)
    lane = lax.broadcasted_iota(I32, (tm, LANES), 1)

    gu = _gelu(_dot(h, wu[...]))
    gv = _gelu(_dot(h, wv[...]))
    mu = jnp.mean(gv, axis=-1, keepdims=True)
    vc = gv - mu
    var = jnp.mean(vc * vc, axis=-1, keepdims=True)
    vln = vc * lax.rsqrt(var + LN_EPS) * lng[...] + lnb[...]
    if chunk == 1:
        vrow_ref[...] = vln
        a_ref[...] = (gu * (vln * wsp[...] + bsp[...])).astype(BF16)
    else:
        vrow_ref[0] = vln[tm - chunk:, :]
        tri = (lax.broadcasted_iota(I32, (chunk, chunk), 0) >= lax.broadcasted_iota(I32, (chunk, chunk), 1))
        vlb = vln.astype(BF16)
        for g in range(A_GROUPS):
            wt = jnp.where(tri, wsp[g], 0.0).astype(BF16)
            cs = slice(g * A_GROUP_DIM, (g + 1) * A_GROUP_DIM)
            for c in range(tm // chunk):
                rs = slice(c * chunk, (c + 1) * chunk)
                sv = _dot(wt, vlb[rs, cs]) + bsp[:, cs]
                a_ref[rs, cs] = (gu[rs, cs] * sv).astype(BF16)

    cos = c128[...]
    sin = s128[...]
    zq = _dot(h, wq[...])
    for hh in range(N_HEADS):
        cs = slice(hh * HEAD_DIM, (hh + 1) * HEAD_DIM)
        q_ref[:, cs] = (_rope_full(zq[:, cs], cos, sin) * Q_SCALE).astype(BF16)
    zk = _dot(h, wk[...])
    for hh in range(N_KV):
        cs = slice(hh * HEAD_DIM, (hh + 1) * HEAD_DIM)
        kr = _rope_full(zk[:, cs], cos, sin)
        if chunk == 1:
            kf_ref[:, cs] = kr
        else:
            kf_ref[layer, 0, :, hh, :] = kr
        kb_ref[:, cs] = kr.astype(BF16)
    zv = _dot(h, wvv[...])
    if chunk == 1:
        vf_ref[...] = zv
        vb_ref[...] = zv.astype(BF16)
    else:
        for hh in range(N_KV):
            vf_ref[layer, 0, :, hh, :] = zv[:, hh * HEAD_DIM:(hh + 1) * HEAD_DIM]
        zvt = zv.T
        for n in range(N_KV):
            vb_ref[0, n, 0:HEAD_DIM, :] = zvt[n * HEAD_DIM:(n + 1) * HEAD_DIM, :].astype(BF16)
            vb_ref[0, n, HEAD_DIM:VT_ROWS, :] = jnp.ones((VT_ROWS - HEAD_DIM, tm), BF16)

    ci = c64[...]
    si = s64[...]
    zqi = _dot(h, wqi[...])
    low = lane < IDX_DIM
    for p in range(IDX_HEADS // 2):
        r = _rope_idx(zqi[:, p * LANES:(p + 1) * LANES], ci, si, lane)
        qi_ref[:, (2 * p) * LANES:(2 * p + 1) * LANES] = jnp.where(low, r, 0.0).astype(BF16)
        qi_ref[:, (2 * p + 1) * LANES:(2 * p + 2) * LANES] = jnp.where(
            low, pltpu.roll(r, IDX_DIM, axis=1), 0.0).astype(BF16)
    zkw = _dot(h, wkw[...])
    kir = _rope_idx(zkw, ci, si, lane)
    wscale = (IDX_HEADS ** -0.5) * (IDX_DIM ** -0.5)
    kw_ref[...] = jnp.where(low, kir, zkw * wscale)
    kwb_ref[...] = jnp.where(low, kir, 0.0).astype(BF16)

    sga_ref[...] = jax.nn.sigmoid(_dot(h, wga[...])).astype(BF16)
    sgb_ref[...] = jax.nn.sigmoid(_dot(h, wgb[...])).astype(BF16)


def _in_proj(x2d, lw, tabs, seq, chunk, kv_all=None):
    m = x2d.shape[0]
    tm = min(ROW_TILE, seq) if chunk != 1 else m
    nt = seq // tm if chunk != 1 else 1
    nb = m // seq if chunk != 1 else 1
    row = lambda w: pl.BlockSpec((tm, w), lambda i: (i, 0))
    tab = pl.BlockSpec((tm, LANES), lambda i: (i % nt, 0))
    if chunk == 1:
        vrow_spec = row(A_WIDTH)
        vrow_shape = jax.ShapeDtypeStruct((m, A_WIDTH), F32)
        sp_specs = [_const_spec((1, A_WIDTH)), _const_spec((1, A_WIDTH))]
    else:
        vrow_spec = pl.BlockSpec((1, chunk, A_WIDTH), lambda i: (i // nt, 0, 0))
        vrow_shape = jax.ShapeDtypeStruct((nb, chunk, A_WIDTH), F32)
        sp_specs = [_const_spec((A_GROUPS, chunk, chunk)), _const_spec((chunk, A_WIDTH))]
    widths = [(A_WIDTH, BF16), None, (N_HEADS * HEAD_DIM, BF16), (KV_WIDTH, F32), (KV_WIDTH, BF16),
              (KV_WIDTH, F32), (KV_WIDTH, BF16), (IDX_HEADS * LANES, BF16), (LANES, F32), (LANES, BF16),
              (D_MODEL, BF16), (D_MODEL, BF16)]
    out_specs = [vrow_spec if w is None else row(w[0]) for w in widths]
    out_shape = [vrow_shape if w is None else jax.ShapeDtypeStruct((m, w[0]), w[1]) for w in widths]
    layer = 0
    prev = (jnp.zeros((8, LANES), F32),) * 2
    prev_specs = [pl.BlockSpec(memory_space=pl.ANY)] * 2
    if chunk != 1:
        out_specs[6] = pl.BlockSpec((1, N_KV, VT_ROWS, tm), lambda i: (i // nt, 0, 0, i % nt))
        out_shape[6] = jax.ShapeDtypeStruct((nb, N_KV, VT_ROWS, seq), BF16)
        layer = kv_all[0]
        kvblk = lambda nl: pl.BlockSpec((nl, 1, tm, N_KV, HEAD_DIM), lambda i: (0, i // nt, i % nt, 0, 0))
        for slot in (3, 5):
            out_specs[slot] = kvblk(layer + 1)
            out_shape[slot] = jax.ShapeDtypeStruct((layer + 1, nb, seq, N_KV, HEAD_DIM), F32)
        if layer > 0:
            prev = (kv_all[1], kv_all[2])
            prev_specs = [kvblk(layer)] * 2
    wshapes = [(D_MODEL, A_WIDTH), (D_MODEL, A_WIDTH), (D_MODEL, N_HEADS * HEAD_DIM), (D_MODEL, KV_WIDTH),
               (D_MODEL, KV_WIDTH), (D_MODEL, IDX_HEADS * IDX_DIM), (D_MODEL, LANES), (D_MODEL, D_MODEL),
               (D_MODEL, D_MODEL)]
    in_specs = ([row(D_MODEL), _const_spec((1, D_MODEL))] + [_const_spec(s) for s in wshapes]
                + [_const_spec((1, A_WIDTH)), _const_spec((1, A_WIDTH))] + sp_specs + [tab] * 4 + prev_specs)
    sp = (lw["sp_w1"], lw["sp_b1"]) if chunk == 1 else (lw["sp_w"], lw["sp_b"])
    return pl.pallas_call(
        functools.partial(_in_proj_kernel, chunk, layer),
        grid=(m // tm,),
        in_specs=in_specs,
        out_specs=out_specs,
        out_shape=out_shape,
        compiler_params=_cparams(("arbitrary",)),
        name="in_proj",
    )(x2d, lw["g0"], lw["wu"], lw["wv"], lw["wq"], lw["wk"], lw["wvv"], lw["wqi"], lw["wkw"], lw["wga"],
      lw["wgb"], lw["ln_g"], lw["ln_b"], sp[0], sp[1], *tabs, *prev)


def _topk_threshold(count_ge, topk, shape):
    t0 = jnp.where(count_ge(jnp.zeros(shape, I32)) >= topk, jnp.int32(0), jnp.int32(INT_MIN))

    def bit_body(b, t):
        cand = t + lax.shift_left(jnp.int32(1), jnp.int32(30) - b)
        return jnp.where(count_ge(cand) >= topk, cand, t)

    return lax.fori_loop(0, 31, bit_body, jnp.broadcast_to(t0, shape))


def _group_sum(x):
    chains = 4
    accs = [x[c * 8:(c + 1) * 8, :] for c in range(chains)]
    for r in range(chains, x.shape[0] // 8):
        accs[r % chains] = accs[r % chains] + x[r * 8:(r + 1) * 8, :]
    return (accs[0] + accs[1]) + (accs[2] + accs[3])


def _dsa_prompt_t_kernel(topk, kb, q_ref, qi_ref, kw_ref, kib_ref, k_ref, vt_ref, o_ref, qis, keys, bias, qn, acc_ref):
    tq = q_ref.shape[1]
    seq = k_ref.shape[1]
    i = pl.program_id(1)
    nkb = ((i + 1) * tq + kb - 1) // kb
    qpos = i * tq + lax.broadcasted_iota(I32, (1, tq), 1)
    krow = lax.broadcasted_iota(I32, (kb, tq), 0)
    shape = (1, tq)

    for h in range(IDX_HEADS):
        qis[h * tq:(h + 1) * tq, :] = qi_ref[0, :, h * LANES:(h + 1) * LANES]
    kwt = kw_ref[0].T
    wi = [kwt[IDX_DIM + h:IDX_DIM + h + 1, :] for h in range(IDX_HEADS)]

    def score_body(j, carry):
        base = pl.multiple_of(j * kb, kb)
        lg = _dot_t(kib_ref[0, pl.ds(base, kb), :], qis[...])
        acc = jnp.zeros((kb, tq), F32)
        for h in range(IDX_HEADS):
            acc = acc + wi[h] * jnp.maximum(lg[:, h * tq:(h + 1) * tq], 0.0)
        keys[pl.ds(base, kb), :] = jnp.where(base + krow <= qpos, _float_key(acc), jnp.int32(INT_MIN))
        return carry

    lax.fori_loop(0, nkb, score_body, 0)

    sub = 64
    srow = lax.broadcasted_iota(I32, (sub, tq), 0)

    def count(pred):
        def body(j, part):
            base = pl.multiple_of(j * kb, kb)
            for r in range(kb // sub):
                lo = pl.multiple_of(base + r * sub, sub)
                hit = jnp.where(pred(keys[pl.ds(lo, sub), :], lo + srow), 1.0, 0.0)
                part = part + _group_sum(hit)
            return part
        part = lax.fori_loop(0, nkb, body, jnp.zeros((8, tq), F32))
        return jnp.sum(part, axis=0, keepdims=True)

    thr = _topk_threshold(lambda cand: count(lambda k, pos: k >= cand), float(topk), shape)
    thr = jnp.maximum(thr, jnp.int32(INT_MIN + 1))

    def write_bias(sel):
        def body(j, carry):
            base = pl.multiple_of(j * kb, kb)
            bias[pl.ds(base, kb), :] = jnp.where(sel(keys[pl.ds(base, kb), :], base + krow), 0.0, NEG_BIAS)
            return carry
        lax.fori_loop(0, nkb, body, 0)

    n_ge = count(lambda k, pos: k >= thr)
    has_ties = jnp.max(n_ge) > float(topk)

    @pl.when(jnp.logical_not(has_ties))
    def _():
        write_bias(lambda k, pos: k >= thr)

    @pl.when(has_ties)
    def _():
        need = float(topk) - count(lambda k, pos: k > thr)
        nbits = int(np.ceil(np.log2(seq))) + 1

        def bit_body(b, p):
            cand = p + lax.shift_left(jnp.int32(1), jnp.int32(nbits - 1) - b)
            below = count(lambda k, pos: jnp.logical_and(k == thr, pos < cand))
            return jnp.where(below < need, cand, p)

        last = lax.fori_loop(0, nbits, bit_body, jnp.zeros(shape, I32))
        write_bias(lambda k, pos: jnp.logical_or(k > thr, jnp.logical_and(k == thr, pos <= last)))

    ngrp = qn.shape[0]
    stack = N_HEADS // ngrp
    for u in range(ngrp):
        for g in range(stack):
            hh = stack * u + g
            qn[u, g * tq:(g + 1) * tq, :] = q_ref[0, :, hh * HEAD_DIM:(hh + 1) * HEAD_DIM]
    acc_ref[...] = jnp.zeros(acc_ref.shape, F32)

    def att_body(j, ms):
        base = pl.multiple_of(j * kb, kb)
        bb = bias[pl.ds(base, kb), :]
        if stack > 1:
            bb = jnp.concatenate([bb] * stack, axis=1)
        new_m = []
        for u in range(ngrp):
            n = (u * stack) // GQA
            s = _dot_t(k_ref[0, pl.ds(base, kb), n * HEAD_DIM:(n + 1) * HEAD_DIM], qn[u]) + bb
            m_new = jnp.maximum(ms[u], jnp.max(s, axis=0, keepdims=True))
            p = jnp.exp2(s - m_new)
            alpha = jnp.exp2(ms[u] - m_new)
            acc_ref[u] = alpha * acc_ref[u] + _dot(vt_ref[0, n, :, pl.ds(base, kb)], p.astype(BF16))
            new_m.append(m_new)
        return tuple(new_m)

    m0 = tuple(jnp.full((1, stack * tq), -jnp.inf, F32) for _ in range(ngrp))
    lax.fori_loop(0, nkb, att_body, m0)
    for u in range(ngrp):
        o = acc_ref[u, 0:HEAD_DIM, :] / acc_ref[u, HEAD_DIM:HEAD_DIM + 1, :]
        for g in range(stack):
            hh = stack * u + g
            o_ref[0, :, hh * HEAD_DIM:(hh + 1) * HEAD_DIM] = o[:, g * tq:(g + 1) * tq].T.astype(BF16)


def _dsa_prompt_t(q, qi, kw, kib, kb16, vt, nb, seq):
    tq = min(QUERY_BLOCK, seq)
    kb = min(KEY_BLOCK, seq)
    topk = min(TOPK_MAX, seq // 4)
    r3 = lambda a: a.reshape(nb, seq, a.shape[-1])
    qblk = lambda w: pl.BlockSpec((1, tq, w), lambda b, i: (b, i, 0))
    full = lambda w: pl.BlockSpec((1, seq, w), lambda b, i: (b, 0, 0))
    out = pl.pallas_call(
        functools.partial(_dsa_prompt_t_kernel, topk, kb),
        grid=(nb, seq // tq),
        in_specs=[qblk(N_HEADS * HEAD_DIM), qblk(IDX_HEADS * LANES), qblk(LANES), full(LANES), full(KV_WIDTH),
                  pl.BlockSpec((1, N_KV, VT_ROWS, seq), lambda b, i: (b, 0, 0, 0))],
        out_specs=qblk(N_HEADS * HEAD_DIM),
        out_shape=jax.ShapeDtypeStruct((nb, seq, N_HEADS * HEAD_DIM), BF16),
        scratch_shapes=[pltpu.VMEM((IDX_HEADS * tq, LANES), BF16), pltpu.VMEM((seq, tq), I32),
                        pltpu.VMEM((seq, tq), F32),
                        pltpu.VMEM((N_HEADS // HEAD_STACK, HEAD_STACK * tq, HEAD_DIM), BF16),
                        pltpu.VMEM((N_HEADS // HEAD_STACK, VT_ROWS, HEAD_STACK * tq), F32)],
        compiler_params=_cparams(("parallel", "arbitrary")),
        name="dsa_prompt",
    )(r3(q), r3(qi), r3(kw), r3(kib), r3(kb16), vt)
    return out.reshape(nb * seq, N_HEADS * HEAD_DIM)


def _dsa_select_kernel(layer, topk, npages, pt_ref, qi_ref, kw_ref, kwb_ref, cik_ref, idx_ref, ibuf, keys, grank, isem):
    b = pl.program_id(0)
    nrow = keys.shape[0]
    group = 8

    def idx_copy(j):
        return pltpu.make_async_copy(cik_ref.at[layer, pt_ref[b, j]], ibuf.at[j], isem)

    def start_idx(j, c):
        idx_copy(j).start()
        return c

    lax.fori_loop(0, npages, start_idx, 0)

    qif = qi_ref[0].astype(F32)
    qi8f = jnp.concatenate([qif[:, h * LANES:h * LANES + IDX_DIM] for h in range(IDX_HEADS)], axis=0)
    qi8 = qi8f.astype(BF16)
    kwrow = kw_ref[0]
    eye = (lax.broadcasted_iota(I32, (IDX_HEADS, LANES), 0) + IDX_DIM
           == lax.broadcasted_iota(I32, (IDX_HEADS, LANES), 1))
    wcol = jnp.sum(jnp.where(eye, jnp.broadcast_to(kwrow, (IDX_HEADS, LANES)), 0.0), axis=1, keepdims=True)

    def wait_idx(j, c):
        idx_copy(j).wait()
        return c

    lax.fori_loop(0, npages, wait_idx, 0)

    def score_body(jj, c):
        rows = []
        for u in range(group):
            lg = _dot(qi8, ibuf[jj * group + u].astype(BF16))
            rows.append(jnp.sum(wcol * jnp.maximum(lg, 0.0), axis=0, keepdims=True))
        tile = jnp.concatenate(rows, axis=0)
        keys[pl.ds(pl.multiple_of(jj * group, group), group), :] = _float_key(tile + 0.0)
        return c

    lax.fori_loop(0, npages // group, score_body, 0)
    lgs = jnp.sum(qi8f * kwb_ref[0].astype(F32)[:, :IDX_DIM], axis=1, keepdims=True)
    scs = jnp.sum(wcol * jnp.maximum(lgs, 0.0), axis=0, keepdims=True)
    tshape = (nrow - npages, LANES)
    first = jnp.logical_and(lax.broadcasted_iota(I32, tshape, 0) == 0, lax.broadcasted_iota(I32, tshape, 1) == 0)
    keys[npages:nrow, :] = jnp.where(first, _float_key(jnp.broadcast_to(scs, tshape) + 0.0), jnp.int32(INT_MIN))

    kall = keys[...]
    pos = lax.broadcasted_iota(I32, (nrow, LANES), 0) * LANES + lax.broadcasted_iota(I32, (nrow, LANES), 1)
    shape = (1, 1)

    def count(pred):
        hit = jnp.where(pred, 1.0, 0.0)
        return jnp.sum(jnp.sum(hit, axis=1, keepdims=True), axis=0, keepdims=True)

    thr = _topk_threshold(lambda cand: count(kall >= cand), float(topk), shape)
    thr = jnp.maximum(thr, jnp.int32(INT_MIN + 1))
    need = float(topk) - count(kall > thr)
    tied = kall == thr
    nbits = int(np.ceil(np.log2(nrow * LANES))) + 1

    def bit_body(bb, p):
        cand = p + lax.shift_left(jnp.int32(1), jnp.int32(nbits - 1) - bb)
        below = count(jnp.logical_and(tied, pos < cand))
        return jnp.where(below < need, cand, p)

    n_gt = float(topk) - need
    n_tied = count(tied)
    surplus = jnp.max(n_gt + n_tied) > float(topk)
    last = lax.cond(surplus,
                    lambda: lax.fori_loop(0, nbits, bit_body, jnp.zeros(shape, I32)),
                    lambda: jnp.full(shape, nrow * LANES, I32))
    sel = jnp.logical_or(kall > thr, jnp.logical_and(tied, pos <= last))
    msk = jnp.where(sel, 1.0, 0.0)

    upper = (lax.broadcasted_iota(I32, (LANES, LANES), 0) <= lax.broadcasted_iota(I32, (LANES, LANES), 1))
    within = _dot(msk.astype(BF16), jnp.where(upper, 1.0, 0.0).astype(BF16))
    rowsum = jnp.broadcast_to(jnp.sum(msk, axis=1, keepdims=True), (nrow, LANES))
    before = (lax.broadcasted_iota(I32, (nrow, nrow), 1) < lax.broadcasted_iota(I32, (nrow, nrow), 0))
    offset = _dot(jnp.where(before, 1.0, 0.0).astype(BF16), rowsum.astype(BF16))
    grank[...] = jnp.where(sel, offset + within - 1.0, -1.0)

    slot = lax.broadcasted_iota(I32, (topk, LANES), 0).astype(F32)
    lane_f = lax.broadcasted_iota(I32, (1, LANES), 1).astype(F32)

    def compact_body(r, acc):
        g = jnp.broadcast_to(grank[pl.ds(r, 1), :], (topk, LANES))
        flat = jnp.broadcast_to(lax.convert_element_type(r * LANES, F32) + lane_f, (topk, LANES))
        return acc + jnp.where(g == slot, flat, 0.0)

    acc = lax.fori_loop(0, npages + 1, compact_body, jnp.zeros((topk, LANES), F32))
    idx_ref[0] = jnp.sum(acc, axis=1, keepdims=True).astype(I32)


def _dsa_attend_kernel(layer, topk, npages, pt_ref, ix_ref, q_ref, kn_ref, vn_ref, ck_ref, cv_ref, o_ref,
                       kg, vg, ksem, vsem):
    b = pl.program_id(0)
    past = npages * PAGE

    def issue(t, c):
        ix = ix_ref[b, t]

        @pl.when(ix < past)
        def _():
            pg = pt_ref[b, lax.shift_right_logical(ix, 7)]
            off = jnp.bitwise_and(ix, PAGE - 1)
            pltpu.make_async_copy(ck_ref.at[layer, pg, off], kg.at[t], ksem).start()
            pltpu.make_async_copy(cv_ref.at[layer, pg, off], vg.at[t], vsem).start()

        @pl.when(ix >= past)
        def _():
            pltpu.make_async_copy(kn_ref.at[b], kg.at[t], ksem).start()
            pltpu.make_async_copy(vn_ref.at[b], vg.at[t], vsem).start()

        return c

    lax.fori_loop(0, topk, issue, 0)

    def wait(t, c):
        pltpu.make_async_copy(kn_ref.at[b], kg.at[t], ksem).wait()
        pltpu.make_async_copy(vn_ref.at[b], vg.at[t], vsem).wait()
        return c

    lax.fori_loop(0, topk, wait, 0)

    qf = q_ref[0].astype(F32)
    pad = jnp.zeros((8 - GQA, HEAD_DIM), F32)
    for n in range(N_KV):
        rows = [qf[:, (GQA * n + g) * HEAD_DIM:(GQA * n + g + 1) * HEAD_DIM] for g in range(GQA)]
        q8 = jnp.concatenate(rows + [pad], axis=0).astype(BF16)
        s = _dot_t(q8, kg[:, n, :].astype(BF16))
        p = jnp.exp2(s - jnp.max(s, axis=1, keepdims=True))
        p = p / jnp.sum(p, axis=1, keepdims=True)
        o = _dot(p.astype(BF16), vg[:, n, :].astype(BF16))
        for g in range(GQA):
            hh = GQA * n + g
            o_ref[0, :, hh * HEAD_DIM:(hh + 1) * HEAD_DIM] = o[g:g + 1, :].astype(BF16)


def _dsa_sample(layer, q, qi, kw, kwb, kf, vf, cache_idx_k, cache_k, cache_v, page_table):
    nb, npages = page_table.shape
    assert npages % 8 == 0 and PAGE == 128
    topk = min(TOPK_MAX, (npages * PAGE + 1) // 4)
    nrow = -(-(npages + 1) // LANES) * LANES
    r3 = lambda a: a.reshape(nb, 1, a.shape[-1])
    anyspec = pl.BlockSpec(memory_space=pl.ANY)
    blk1 = lambda w: pl.BlockSpec((1, 1, w), lambda b, pt: (b, 0, 0))
    idx = pl.pallas_call(
        functools.partial(_dsa_select_kernel, layer, topk, npages),
        grid_spec=pltpu.PrefetchScalarGridSpec(
            num_scalar_prefetch=1,
            grid=(nb,),
            in_specs=[blk1(IDX_HEADS * LANES), blk1(LANES), blk1(LANES), anyspec],
            out_specs=pl.BlockSpec((1, topk, 1), lambda b, pt: (b, 0, 0)),
            scratch_shapes=[pltpu.VMEM((npages, IDX_DIM, PAGE), F32), pltpu.VMEM((nrow, LANES), I32),
                            pltpu.VMEM((nrow, LANES), F32), pltpu.SemaphoreType.DMA(())],
        ),
        out_shape=jax.ShapeDtypeStruct((nb, topk, 1), I32),
        compiler_params=_cparams(("arbitrary",)),
        name="dsa_select",
    )(page_table, r3(qi), r3(kw), r3(kwb), jnp.swapaxes(cache_idx_k, 2, 3))
    blk2 = lambda w: pl.BlockSpec((1, 1, w), lambda b, pt, ix: (b, 0, 0))
    rowshape = (topk, N_KV, HEAD_DIM)
    out = pl.pallas_call(
        functools.partial(_dsa_attend_kernel, layer, topk, npages),
        grid_spec=pltpu.PrefetchScalarGridSpec(
            num_scalar_prefetch=2,
            grid=(nb,),
            in_specs=[blk2(N_HEADS * HEAD_DIM), anyspec, anyspec, anyspec, anyspec],
            out_specs=blk2(N_HEADS * HEAD_DIM),
            scratch_shapes=[pltpu.VMEM(rowshape, F32), pltpu.VMEM(rowshape, F32),
                            pltpu.SemaphoreType.DMA(()), pltpu.SemaphoreType.DMA(())],
        ),
        out_shape=jax.ShapeDtypeStruct((nb, 1, N_HEADS * HEAD_DIM), BF16),
        compiler_params=_cparams(("arbitrary",)),
        name="dsa_attend",
    )(page_table, idx.reshape(nb, topk), r3(q), kf.reshape(nb, N_KV, HEAD_DIM), vf.reshape(nb, N_KV, HEAD_DIM),
      cache_k, cache_v)
    return out.reshape(nb, N_HEADS * HEAD_DIM)


def _mix_out_kernel(a_ref, o_ref, sga_ref, sgb_ref, x_ref, wba, wbb, wout, g_ref, y_ref):
    merged = (sga_ref[...].astype(F32) * _dot(a_ref[...], wba[...])
              + sgb_ref[...].astype(F32) * _dot(o_ref[...], wbb[...]))
    mix = _dot(merged.astype(BF16), wout[...])
    y_ref[...] = x_ref[...] + _rms(mix, g_ref[...])


def _mix_out(a, o, sga, sgb, x2d, lw):
    m = x2d.shape[0]
    tm = min(ROW_TILE, m)
    row = lambda w: pl.BlockSpec((tm, w), lambda i: (i, 0))
    return pl.pallas_call(
        _mix_out_kernel,
        grid=(m // tm,),
        in_specs=[row(A_WIDTH), row(D_MODEL), row(D_MODEL), row(D_MODEL), row(D_MODEL),
                  _const_spec((A_WIDTH, D_MODEL)), _const_spec((D_MODEL, D_MODEL)), _const_spec((D_MODEL, D_MODEL)),
                  _const_spec((1, D_MODEL))],
        out_specs=row(D_MODEL),
        out_shape=jax.ShapeDtypeStruct((m, D_MODEL), F32),
        compiler_params=_cparams(("parallel",)),
        name="mix_out",
    )(a, o, sga, sgb, x2d, lw["wba"], lw["wbb"], lw["wout"], lw["g1"])


def _cross_prompt_kernel(x_ref, g2, g3, wq, wo, mk_ref, mv_ref, y_ref):
    x = x_ref[0]
    h = _rms(x, g2[...]).astype(BF16)
    qm = (_dot(h, wq[...]) * (MEM_HEAD_DIM ** -0.5)).astype(BF16)
    outs = []
    for hh in range(MEM_HEADS):
        cs = slice(hh * MEM_HEAD_DIM, (hh + 1) * MEM_HEAD_DIM)
        s = _dot_t(qm[:, cs], mk_ref[0, :, cs])
        p = jnp.exp(s - jnp.max(s, axis=1, keepdims=True))
        p = p / jnp.sum(p, axis=1, keepdims=True)
        outs.append(_dot(p.astype(BF16), mv_ref[0, :, cs]).astype(BF16))
    c = _dot(jnp.concatenate(outs, axis=1), wo[...])
    y_ref[0] = x + _rms(c, g3[...])


def _cross_prompt(x2d, mk, mv, lw, nb, seq):
    tm = min(ROW_TILE, seq)
    mt = mk.shape[0] // nb
    xb = pl.BlockSpec((1, tm, D_MODEL), lambda b, i: (b, i, 0))
    mb = pl.BlockSpec((1, mt, MEM_WIDTH), lambda b, i: (b, 0, 0))
    out = pl.pallas_call(
        _cross_prompt_kernel,
        grid=(nb, seq // tm),
        in_specs=[xb, _const_spec((1, D_MODEL)), _const_spec((1, D_MODEL)), _const_spec((D_MODEL, MEM_WIDTH)),
                  _const_spec((MEM_WIDTH, D_MODEL)), mb, mb],
        out_specs=xb,
        out_shape=jax.ShapeDtypeStruct((nb, seq, D_MODEL), F32),
        compiler_params=_cparams(("parallel", "parallel")),
        name="cross_prompt",
    )(x2d.reshape(nb, seq, D_MODEL), lw["g2"], lw["g3"], lw["wmq"], lw["wmo"],
      mk.reshape(nb, mt, MEM_WIDTH), mv.reshape(nb, mt, MEM_WIDTH))
    return out.reshape(nb * seq, D_MODEL)


def _cross_sample_kernel(x_ref, g2, g3, wq, wo, mk_ref, mv_ref, y_ref):
    x = x_ref[...]
    bt = x.shape[0]
    h = _rms(x, g2[...]).astype(BF16)
    qm = (_dot(h, wq[...]) * (MEM_HEAD_DIM ** -0.5)).astype(BF16)
    rows = []
    for b in range(bt):
        qb = jnp.broadcast_to(qm[b:b + 1, :], (8, MEM_WIDTH))
        outs = []
        for hh in range(MEM_HEADS):
            cs = slice(hh * MEM_HEAD_DIM, (hh + 1) * MEM_HEAD_DIM)
            s = _dot_t(qb[:, cs], mk_ref[0, b, :, hh, :].astype(BF16))
            p = jnp.exp(s - jnp.max(s, axis=1, keepdims=True))
            p = p / jnp.sum(p, axis=1, keepdims=True)
            outs.append(_dot(p.astype(BF16), mv_ref[0, b, :, hh, :].astype(BF16)))
        rows.append(jnp.concatenate(outs, axis=1)[0:1, :])
    att = jnp.concatenate(rows, axis=0).astype(BF16)
    c = _dot(att, wo[...])
    y_ref[...] = x + _rms(c, g3[...])


def _cross_sample(layer, x2d, mk, mv, lw):
    nb = x2d.shape[0]
    bt = 8
    mt = mk.shape[2]
    xb = pl.BlockSpec((bt, D_MODEL), lambda i: (i, 0))
    mb = pl.BlockSpec((1, bt, mt, MEM_HEADS, MEM_HEAD_DIM), lambda i: (layer, i, 0, 0, 0))
    return pl.pallas_call(
        _cross_sample_kernel,
        grid=(nb // bt,),
        in_specs=[xb, _const_spec((1, D_MODEL)), _const_spec((1, D_MODEL)), _const_spec((D_MODEL, MEM_WIDTH)),
                  _const_spec((MEM_WIDTH, D_MODEL)), mb, mb],
        out_specs=xb,
        out_shape=jax.ShapeDtypeStruct((nb, D_MODEL), F32),
        compiler_params=_cparams(("parallel",)),
        name="cross_sample",
    )(x2d, lw["g2"], lw["g3"], lw["wmq"], lw["wmo"], mk, mv)


def _ffn_seq_kernel(x_ref, g4, g5, wup, cw, cb, wdn, st_ref, y_ref, ns_ref, carry, ext, acc):
    tm = x_ref.shape[1]
    t = pl.program_id(1)

    @pl.when(t == 0)
    def _():
        carry[6:8, :] = st_ref[0]

    x = x_ref[0]
    h = _rms(x, g4[...]).astype(BF16)
    for c in range(D_FF // FF_CHUNK):
        prod = None
        for half in range(2):
            cs = slice(half * D_FF + c * FF_CHUNK, half * D_FF + (c + 1) * FF_CHUNK)
            up = _dot(h, wup[:, cs])
            ext[half, 0:8, :] = carry[:, cs]
            ext[half, 8:8 + tm, :] = up
            carry[:, cs] = up[tm - 8:, :]
            conv = (cb[:, cs] + cw[0:1, cs] * ext[half, 6:6 + tm, :] + cw[1:2, cs] * ext[half, 7:7 + tm, :]
                    + cw[2:3, cs] * up)
            prod = _gelu(conv) if half == 0 else prod * conv
        part = _dot(prod.astype(BF16), wdn[c * FF_CHUNK:(c + 1) * FF_CHUNK, :])
        if c == 0:
            acc[...] = part
        else:
            acc[...] += part
    y_ref[0] = x + _rms(acc[...], g5[...])
    ns_ref[0] = carry[6:8, :]


def _ffn_seq(x2d, state, lw, nb, seq):
    tm = min(ROW_TILE, seq)
    xb = pl.BlockSpec((1, tm, D_MODEL), lambda b, t: (b, t, 0))
    sb = pl.BlockSpec((1, CONV_WIDTH - 1, 2 * D_FF), lambda b, t: (b, 0, 0))
    y, ns = pl.pallas_call(
        _ffn_seq_kernel,
        grid=(nb, seq // tm),
        in_specs=[xb, _const_spec((1, D_MODEL)), _const_spec((1, D_MODEL)), _const_spec((D_MODEL, 2 * D_FF)),
                  _const_spec((CONV_WIDTH, 2 * D_FF)), _const_spec((1, 2 * D_FF)), _const_spec((D_FF, D_MODEL)), sb],
        out_specs=[xb, sb],
        out_shape=[jax.ShapeDtypeStruct((nb, seq, D_MODEL), F32),
                   jax.ShapeDtypeStruct((nb, CONV_WIDTH - 1, 2 * D_FF), F32)],
        scratch_shapes=[pltpu.VMEM((8, 2 * D_FF), F32), pltpu.VMEM((2, tm + 8, FF_CHUNK), F32),
                        pltpu.VMEM((tm, D_MODEL), F32)],
        compiler_params=_cparams(("arbitrary", "arbitrary")),
        name="ffn_seq",
    )(x2d.reshape(nb, seq, D_MODEL), lw["g4"], lw["g5"], lw["wup"], lw["cw"], lw["cb"], lw["wdn"], state)
    return y.reshape(nb * seq, D_MODEL), ns


def _ffn_step_kernel(x_ref, g4, g5, wup, cw, cb, wdn, s0_ref, s1_ref, y_ref, up_ref):
    x = x_ref[...]
    h = _rms(x, g4[...]).astype(BF16)
    acc = None
    for c in range(D_FF // FF_CHUNK):
        prod = None
        for half in range(2):
            cs = slice(half * D_FF + c * FF_CHUNK, half * D_FF + (c + 1) * FF_CHUNK)
            up = _dot(h, wup[:, cs])
            up_ref[:, cs] = up
            conv = cb[:, cs] + cw[0:1, cs] * s0_ref[:, cs] + cw[1:2, cs] * s1_ref[:, cs] + cw[2:3, cs] * up
            prod = _gelu(conv) if half == 0 else prod * conv
        part = _dot(prod.astype(BF16), wdn[c * FF_CHUNK:(c + 1) * FF_CHUNK, :])
        acc = part if acc is None else acc + part
    y_ref[...] = x + _rms(acc, g5[...])


def _ffn_step(x2d, state, lw):
    nb = x2d.shape[0]
    s0 = state[:, 0, :]
    s1 = state[:, 1, :]
    full = lambda w: pl.BlockSpec((nb, w), lambda i: (0, 0))
    y, up = pl.pallas_call(
        _ffn_step_kernel,
        grid=(1,),
        in_specs=[full(D_MODEL), _const_spec((1, D_MODEL)), _const_spec((1, D_MODEL)),
                  _const_spec((D_MODEL, 2 * D_FF)), _const_spec((CONV_WIDTH, 2 * D_FF)), _const_spec((1, 2 * D_FF)),
                  _const_spec((D_FF, D_MODEL)), full(2 * D_FF), full(2 * D_FF)],
        out_specs=[full(D_MODEL), full(2 * D_FF)],
        out_shape=[jax.ShapeDtypeStruct((nb, D_MODEL), F32), jax.ShapeDtypeStruct((nb, 2 * D_FF), F32)],
        compiler_params=_cparams(("arbitrary",)),
        name="ffn_step",
    )(x2d, lw["g4"], lw["g5"], lw["wup"], lw["cw"], lw["cb"], lw["wdn"], s0, s1)
    return y, jnp.stack([s1, up], axis=1)


def _rope_tables(pos):
    def tab(dim, reps):
        half = dim // 2
        inv = ROPE_THETA ** (-jnp.arange(half, dtype=F32) / half)
        ang = pos.astype(F32)[:, None] * inv[None, :]
        cos = jnp.concatenate([jnp.cos(ang)] * 2, axis=1)
        sin = jnp.concatenate([-jnp.sin(ang), jnp.sin(ang)], axis=1)
        return jnp.tile(cos, (1, reps)), jnp.tile(sin, (1, reps))
    c128, s128 = tab(HEAD_DIM, 1)
    c64, s64 = tab(IDX_DIM, LANES // IDX_DIM)
    return c128, s128, c64, s64


def _layer_weights(l, norms, w_in, ln_v_g, ln_v_b, w_spatial, b_spatial, w_branch_a, w_branch_b, w_out, w_mem_q,
                   w_mem_kv, w_mem_o, w_up, conv_w, conv_b, w_down):
    offs = np.cumsum((0,) + IN_SIZES)
    seg = lambda i: w_in[l][:, offs[i]:offs[i + 1]].astype(BF16)
    wkw = jnp.concatenate([w_in[l][:, offs[6]:offs[8]],
                           jnp.zeros((D_MODEL, LANES - IDX_DIM - IDX_HEADS), F32)], axis=1).astype(BF16)
    lw = dict(
        wu=seg(0), wv=seg(1), wq=seg(2), wk=seg(3), wvv=seg(4), wqi=seg(5), wkw=wkw, wga=seg(8), wgb=seg(9),
        ln_g=ln_v_g[l][None, :], ln_b=ln_v_b[l][None, :],
        sp_w=w_spatial[l], sp_b=jnp.repeat(b_spatial[l].T, A_GROUP_DIM, axis=1),
        sp_w1=jnp.repeat(w_spatial[l][:, 0, 0], A_GROUP_DIM)[None, :],
        sp_b1=jnp.repeat(b_spatial[l][:, 0], A_GROUP_DIM)[None, :],
        wba=w_branch_a[l].astype(BF16), wbb=w_branch_b[l].astype(BF16), wout=w_out[l].astype(BF16),
        wmq=w_mem_q[l].astype(BF16), wmkv=w_mem_kv[l].astype(BF16), wmo=w_mem_o[l].astype(BF16),
        wup=w_up[l].astype(BF16), cw=conv_w[l], cb=conv_b[l][None, :], wdn=w_down[l].astype(BF16),
    )
    for i in range(6):
        lw["g%d" % i] = norms[l, i][None, :]
    return lw


def kernel(x_prompt, x_sample, cache_k, cache_v, cache_idx_k, cache_mem_k, cache_mem_v, state_conv, page_table,
           mem_prompt, norms, w_in, ln_v_g, ln_v_b, w_spatial, b_spatial, w_branch_a, w_branch_b, w_out,
           w_mem_q, w_mem_kv, w_mem_o, w_up, conv_w, conv_b, w_down):
    nb, seq, _ = x_prompt.shape
    nd, tdec, _ = x_sample.shape
    assert tdec == 1 and seq % CHUNK == 0
    depth = norms.shape[0]
    mt = mem_prompt.shape[1]
    past = page_table.shape[1] * PAGE

    tabs_p = _rope_tables(jnp.arange(seq))
    tabs_s = tuple(jnp.broadcast_to(t, (nd, LANES)) for t in _rope_tables(jnp.full((1,), past)))
    yp = x_prompt.reshape(nb * seq, D_MODEL)
    ys = x_sample.reshape(nd, D_MODEL)
    mem2d = mem_prompt.reshape(nb * mt, D_MODEL)
    zero_state = jnp.zeros((nb, CONV_WIDTH - 1, 2 * D_FF), F32)

    outs = [[] for _ in range(12)]
    kp_all = vp_all = None
    for l in range(depth):
        lw = _layer_weights(l, norms, w_in, ln_v_g, ln_v_b, w_spatial, b_spatial, w_branch_a, w_branch_b, w_out,
                            w_mem_q, w_mem_kv, w_mem_o, w_up, conv_w, conv_b, w_down)
        mkf, mvf, mkb, mvb = _memkv(mem2d, lw["wmkv"])
        a, vrow, q, kp_all, kb16, vp_all, vt16, qi, kw, kwb, sga, sgb = _in_proj(
            yp, lw, tabs_p, seq, CHUNK, (l, kp_all, vp_all))
        o = _dsa_prompt_t(q, qi, kw, kwb, kb16, vt16, nb, seq)
        yp = _mix_out(a, o, sga, sgb, yp, lw)
        yp = _cross_prompt(yp, mkb, mvb, lw, nb, seq)
        yp, cp = _ffn_seq(yp, zero_state, lw, nb, seq)
        outs[2].append(kw[:, :IDX_DIM].reshape(nb, seq, IDX_DIM))
        outs[6].append(mkf.reshape(nb, mt, MEM_HEADS, MEM_HEAD_DIM))
        outs[7].append(mvf.reshape(nb, mt, MEM_HEADS, MEM_HEAD_DIM))
        outs[8].append(vrow)
        outs[10].append(cp)
        a, vrow, q, kf, kb16, vf, vb16, qi, kw, kwb, sga, sgb = _in_proj(ys, lw, tabs_s, 1, 1)
        o = _dsa_sample(l, q, qi, kw, kwb, kf, vf, cache_idx_k, cache_k, cache_v, page_table)
        ys = _mix_out(a, o, sga, sgb, ys, lw)
        ys = _cross_sample(l, ys, cache_mem_k, cache_mem_v, lw)
        ys, cs = _ffn_step(ys, state_conv[l], lw)
        outs[3].append(kf.reshape(nd, 1, N_KV, HEAD_DIM))
        outs[4].append(vf.reshape(nd, 1, N_KV, HEAD_DIM))
        outs[5].append(kw[:, :IDX_DIM].reshape(nd, 1, IDX_DIM))
        outs[9].append(vrow.reshape(nd, 1, A_WIDTH))
        outs[11].append(cs)
    stacked = [jnp.stack(o) if o else None for o in outs]
    stacked[0], stacked[1] = kp_all, vp_all
    return (yp.reshape(nb, seq, D_MODEL), ys.reshape(nd, 1, D_MODEL)) + tuple(stacked)
```

```python
import functools

import numpy as np
import jax
import jax.numpy as jnp
from jax import lax
from jax.experimental import pallas as pl
from jax.experimental.pallas import tpu as pltpu

F32 = jnp.float32
BF16 = jnp.bfloat16
I32 = jnp.int32

D_MODEL = 1024
PAGE = 128
CHUNK = 128
A_GROUPS = 4
A_GROUP_DIM = 128
A_WIDTH = A_GROUPS * A_GROUP_DIM
N_HEADS = 8
N_KV = 4
HEAD_DIM = 128
GQA = N_HEADS // N_KV
KV_WIDTH = N_KV * HEAD_DIM
IDX_HEADS = 8
IDX_DIM = 64
TOPK_MAX = 256
ROPE_THETA = 10000.0
MEM_HEADS = 4
MEM_HEAD_DIM = 128
MEM_WIDTH = MEM_HEADS * MEM_HEAD_DIM
D_FF = 2816
CONV_WIDTH = 3
RMS_EPS = 1e-6
LN_EPS = 1e-5
IN_SIZES = (A_WIDTH, A_WIDTH, N_HEADS * HEAD_DIM, KV_WIDTH, KV_WIDTH,
            IDX_HEADS * IDX_DIM, IDX_DIM, IDX_HEADS, D_MODEL, D_MODEL)

LANES = 128
ROW_TILE = 512
QUERY_BLOCK = 256
KEY_BLOCK = 512
HEAD_STACK = 2
VT_ROWS = HEAD_DIM + 16
FF_CHUNK = 2816
VMEM_LIMIT = 56 * 1024 * 1024
Q_SCALE = (HEAD_DIM ** -0.5) * 1.4426950408889634
INT_MIN = -2147483648
NEG_BIAS = -1e30


def _cparams(sem):
    return pltpu.CompilerParams(dimension_semantics=sem, vmem_limit_bytes=VMEM_LIMIT)


def _const_spec(shape):
    nd = len(shape)
    return pl.BlockSpec(shape, lambda *_: (0,) * nd, pipeline_mode=pl.Buffered(1))


def _gelu(x):
    return 0.5 * x * (1.0 + jnp.tanh(0.7978845608028654 * (x + 0.044715 * (x * x * x))))


def _rms(x, g):
    return x * lax.rsqrt(jnp.mean(x * x, axis=-1, keepdims=True) + RMS_EPS) * g


def _dot(a, b):
    return jnp.dot(a, b, preferred_element_type=F32)


def _dot_t(a, b):
    return lax.dot_general(a, b, (((1,), (1,)), ((), ())), preferred_element_type=F32)


def _float_key(x):
    b = pltpu.bitcast(x, I32)
    return jnp.where(b < 0, b ^ jnp.int32(0x7FFFFFFF), b)


def _memkv_kernel(x_ref, w_ref, kf_ref, vf_ref, kb_ref, vb_ref):
    z = _dot(x_ref[...].astype(BF16), w_ref[...])
    k = z[:, :MEM_WIDTH]
    v = z[:, MEM_WIDTH:]
    kf_ref[...] = k
    vf_ref[...] = v
    kb_ref[...] = k.astype(BF16)
    vb_ref[...] = v.astype(BF16)


def _memkv(mem2d, w_bf):
    m = mem2d.shape[0]
    tm = min(ROW_TILE, m)
    out = [jax.ShapeDtypeStruct((m, MEM_WIDTH), F32)] * 2 + [jax.ShapeDtypeStruct((m, MEM_WIDTH), BF16)] * 2
    row = lambda w: pl.BlockSpec((tm, w), lambda i: (i, 0))
    return pl.pallas_call(
        _memkv_kernel,
        grid=(m // tm,),
        in_specs=[row(D_MODEL), _const_spec((D_MODEL, 2 * MEM_WIDTH))],
        out_specs=[row(MEM_WIDTH)] * 4,
        out_shape=out,
        compiler_params=_cparams(("parallel",)),
        name="memkv",
    )(mem2d, w_bf)


def _rope_full(z, cos, sin_signed):
    return z * cos + pltpu.roll(z, HEAD_DIM // 2, axis=1) * sin_signed


def _rope_idx(z, cos, sin_signed, lane):
    partner = jnp.where((lane % IDX_DIM) < IDX_DIM // 2,
                        pltpu.roll(z, LANES - IDX_DIM // 2, axis=1),
                        pltpu.roll(z, IDX_DIM // 2, axis=1))
    return z * cos + partner * sin_signed


def _in_proj_kernel(chunk, layer, x_ref, g_ref, wu, wv, wq, wk, wvv, wqi, wkw, wga, wgb, lng, lnb, wsp, bsp,
                    c128, s128, c64, s64, kprev_ref, vprev_ref,
                    a_ref, vrow_ref, q_ref, kf_ref, kb_ref, vf_ref, vb_ref, qi_ref, kw_ref, kwb_ref,
                    sga_ref, sgb_ref):
    if layer > 0:
        kf_ref[0:layer] = kprev_ref[...]
        vf_ref[0:layer] = vprev_ref[...]
    tm = x_ref.shape[0]
    h = _rms(x_ref[...], g_ref[...]).astype(BF16)
    lane = lax.broadcasted_iota(I32, (tm, LANES), 1)

    gu = _gelu(_dot(h, wu[...]))
    gv = _gelu(_dot(h, wv[...]))
    mu = jnp.mean(gv, axis=-1, keepdims=True)
    vc = gv - mu
    var = jnp.mean(vc * vc, axis=-1, keepdims=True)
    vln = vc * lax.rsqrt(var + LN_EPS) * lng[...] + lnb[...]
    if chunk == 1:
        vrow_ref[...] = vln
        a_ref[...] = (gu * (vln * wsp[...] + bsp[...])).astype(BF16)
    else:
        vrow_ref[0] = vln[tm - chunk:, :]
        tri = (lax.broadcasted_iota(I32, (chunk, chunk), 0) >= lax.broadcasted_iota(I32, (chunk, chunk), 1))
        vlb = vln.astype(BF16)
        for g in range(A_GROUPS):
            wt = jnp.where(tri, wsp[g], 0.0).astype(BF16)
            cs = slice(g * A_GROUP_DIM, (g + 1) * A_GROUP_DIM)
            for c in range(tm // chunk):
                rs = slice(c * chunk, (c + 1) * chunk)
                sv = _dot(wt, vlb[rs, cs]) + bsp[:, cs]
                a_ref[rs, cs] = (gu[rs, cs] * sv).astype(BF16)

    cos = c128[...]
    sin = s128[...]
    zq = _dot(h, wq[...])
    for hh in range(N_HEADS):
        cs = slice(hh * HEAD_DIM, (hh + 1) * HEAD_DIM)
        q_ref[:, cs] = (_rope_full(zq[:, cs], cos, sin) * Q_SCALE).astype(BF16)
    zk = _dot(h, wk[...])
    for hh in range(N_KV):
        cs = slice(hh * HEAD_DIM, (hh + 1) * HEAD_DIM)
        kr = _rope_full(zk[:, cs], cos, sin)
        if chunk == 1:
            kf_ref[:, cs] = kr
        else:
            kf_ref[layer, 0, :, hh, :] = kr
        kb_ref[:, cs] = kr.astype(BF16)
    zv = _dot(h, wvv[...])
    if chunk == 1:
        vf_ref[...] = zv
        vb_ref[...] = zv.astype(BF16)
    else:
        for hh in range(N_KV):
            vf_ref[layer, 0, :, hh, :] = zv[:, hh * HEAD_DIM:(hh + 1) * HEAD_DIM]
        zvt = zv.T
        for n in range(N_KV):
            vb_ref[0, n, 0:HEAD_DIM, :] = zvt[n * HEAD_DIM:(n + 1) * HEAD_DIM, :].astype(BF16)
            vb_ref[0, n, HEAD_DIM:VT_ROWS, :] = jnp.ones((VT_ROWS - HEAD_DIM, tm), BF16)

    ci = c64[...]
    si = s64[...]
    zqi = _dot(h, wqi[...])
    low = lane < IDX_DIM
    for p in range(IDX_HEADS // 2):
        r = _rope_idx(zqi[:, p * LANES:(p + 1) * LANES], ci, si, lane)
        qi_ref[:, (2 * p) * LANES:(2 * p + 1) * LANES] = jnp.where(low, r, 0.0).astype(BF16)
        qi_ref[:, (2 * p + 1) * LANES:(2 * p + 2) * LANES] = jnp.where(
            low, pltpu.roll(r, IDX_DIM, axis=1), 0.0).astype(BF16)
    zkw = _dot(h, wkw[...])
    kir = _rope_idx(zkw, ci, si, lane)
    wscale = (IDX_HEADS ** -0.5) * (IDX_DIM ** -0.5)
    kw_ref[...] = jnp.where(low, kir, zkw * wscale)
    kwb_ref[...] = jnp.where(low, kir, 0.0).astype(BF16)

    sga_ref[...] = jax.nn.sigmoid(_dot(h, wga[...])).astype(BF16)
    sgb_ref[...] = jax.nn.sigmoid(_dot(h, wgb[...])).astype(BF16)


def _in_proj(x2d, lw, tabs, seq, chunk, kv_all=None):
    m = x2d.shape[0]
    tm = min(ROW_TILE, seq) if chunk != 1 else m
    nt = seq // tm if chunk != 1 else 1
    nb = m // seq if chunk != 1 else 1
    row = lambda w: pl.BlockSpec((tm, w), lambda i: (i, 0))
    tab = pl.BlockSpec((tm, LANES), lambda i: (i % nt, 0))
    if chunk == 1:
        vrow_spec = row(A_WIDTH)
        vrow_shape = jax.ShapeDtypeStruct((m, A_WIDTH), F32)
        sp_specs = [_const_spec((1, A_WIDTH)), _const_spec((1, A_WIDTH))]
    else:
        vrow_spec = pl.BlockSpec((1, chunk, A_WIDTH), lambda i: (i // nt, 0, 0))
        vrow_shape = jax.ShapeDtypeStruct((nb, chunk, A_WIDTH), F32)
        sp_specs = [_const_spec((A_GROUPS, chunk, chunk)), _const_spec((chunk, A_WIDTH))]
    widths = [(A_WIDTH, BF16), None, (N_HEADS * HEAD_DIM, BF16), (KV_WIDTH, F32), (KV_WIDTH, BF16),
              (KV_WIDTH, F32), (KV_WIDTH, BF16), (IDX_HEADS * LANES, BF16), (LANES, F32), (LANES, BF16),
              (D_MODEL, BF16), (D_MODEL, BF16)]
    out_specs = [vrow_spec if w is None else row(w[0]) for w in widths]
    out_shape = [vrow_shape if w is None else jax.ShapeDtypeStruct((m, w[0]), w[1]) for w in widths]
    layer = 0
    prev = (jnp.zeros((8, LANES), F32),) * 2
    prev_specs = [pl.BlockSpec(memory_space=pl.ANY)] * 2
    if chunk != 1:
        out_specs[6] = pl.BlockSpec((1, N_KV, VT_ROWS, tm), lambda i: (i // nt, 0, 0, i % nt))
        out_shape[6] = jax.ShapeDtypeStruct((nb, N_KV, VT_ROWS, seq), BF16)
        layer = kv_all[0]
        kvblk = lambda nl: pl.BlockSpec((nl, 1, tm, N_KV, HEAD_DIM), lambda i: (0, i // nt, i % nt, 0, 0))
        for slot in (3, 5):
            out_specs[slot] = kvblk(layer + 1)
            out_shape[slot] = jax.ShapeDtypeStruct((layer + 1, nb, seq, N_KV, HEAD_DIM), F32)
        if layer > 0:
            prev = (kv_all[1], kv_all[2])
            prev_specs = [kvblk(layer)] * 2
    wshapes = [(D_MODEL, A_WIDTH), (D_MODEL, A_WIDTH), (D_MODEL, N_HEADS * HEAD_DIM), (D_MODEL, KV_WIDTH),
               (D_MODEL, KV_WIDTH), (D_MODEL, IDX_HEADS * IDX_DIM), (D_MODEL, LANES), (D_MODEL, D_MODEL),
               (D_MODEL, D_MODEL)]
    in_specs = ([row(D_MODEL), _const_spec((1, D_MODEL))] + [_const_spec(s) for s in wshapes]
                + [_const_spec((1, A_WIDTH)), _const_spec((1, A_WIDTH))] + sp_specs + [tab] * 4 + prev_specs)
    sp = (lw["sp_w1"], lw["sp_b1"]) if chunk == 1 else (lw["sp_w"], lw["sp_b"])
    return pl.pallas_call(
        functools.partial(_in_proj_kernel, chunk, layer),
        grid=(m // tm,),
        in_specs=in_specs,
        out_specs=out_specs,
        out_shape=out_shape,
        compiler_params=_cparams(("arbitrary",)),
        name="in_proj",
    )(x2d, lw["g0"], lw["wu"], lw["wv"], lw["wq"], lw["wk"], lw["wvv"], lw["wqi"], lw["wkw"], lw["wga"],
      lw["wgb"], lw["ln_g"], lw["ln_b"], sp[0], sp[1], *tabs, *prev)


def _topk_threshold(count_ge, topk, shape):
    t0 = jnp.where(count_ge(jnp.zeros(shape, I32)) >= topk, jnp.int32(0), jnp.int32(INT_MIN))

    def bit_body(b, t):
        cand = t + lax.shift_left(jnp.int32(1), jnp.int32(30) - b)
        return jnp.where(count_ge(cand) >= topk, cand, t)

    return lax.fori_loop(0, 31, bit_body, jnp.broadcast_to(t0, shape))


def _group_sum(x):
    chains = 4
    accs = [x[c * 8:(c + 1) * 8, :] for c in range(chains)]
    for r in range(chains, x.shape[0] // 8):
        accs[r % chains] = accs[r % chains] + x[r * 8:(r + 1) * 8, :]
    return (accs[0] + accs[1]) + (accs[2] + accs[3])


def _group_sum16(x):
    chains = 4
    accs = [x[c * 16:(c + 1) * 16, :] for c in range(chains)]
    for r in range(chains, x.shape[0] // 16):
        accs[r % chains] = accs[r % chains] + x[r * 16:(r + 1) * 16, :]
    return (accs[0] + accs[1]) + (accs[2] + accs[3])


def _dsa_prompt_t_kernel(topk, kb, q_ref, qi_ref, kw_ref, kib_ref, k_ref, vt_ref, o_ref, qis, keys, keys16, bias, qn,
                         acc_ref):
    tq = q_ref.shape[1]
    seq = k_ref.shape[1]
    i = pl.program_id(1)
    nkb = ((i + 1) * tq + kb - 1) // kb
    qpos = i * tq + lax.broadcasted_iota(I32, (1, tq), 1)
    krow = lax.broadcasted_iota(I32, (kb, tq), 0)
    shape = (1, tq)

    for h in range(IDX_HEADS):
        qis[h * tq:(h + 1) * tq, :] = qi_ref[0, :, h * LANES:(h + 1) * LANES]
    kwt = kw_ref[0].T
    wi = [kwt[IDX_DIM + h:IDX_DIM + h + 1, :] for h in range(IDX_HEADS)]

    def score_body(j, carry):
        base = pl.multiple_of(j * kb, kb)
        lg = _dot_t(kib_ref[0, pl.ds(base, kb), :], qis[...])
        acc = jnp.zeros((kb, tq), F32)
        for h in range(IDX_HEADS):
            acc = acc + wi[h] * jnp.maximum(lg[:, h * tq:(h + 1) * tq], 0.0)
        adm = base + krow <= qpos
        keys[pl.ds(base, kb), :] = jnp.where(adm, _float_key(acc), jnp.int32(INT_MIN))
        top = pltpu.bitcast(pltpu.bitcast(acc, I32) & jnp.int32(-65536), F32)
        keys16[pl.ds(base, kb), :] = jnp.where(adm, top, -jnp.inf).astype(BF16)
        return carry

    lax.fori_loop(0, nkb, score_body, 0)

    sub = 64
    srow = lax.broadcasted_iota(I32, (sub, tq), 0)
    sub16 = 128

    def count16(cand):
        def body(j, part):
            base = pl.multiple_of(j * kb, kb)
            for r in range(kb // sub16):
                lo = pl.multiple_of(base + r * sub16, sub16)
                hit = jnp.where(keys16[pl.ds(lo, sub16), :] >= cand, jnp.ones((), BF16), jnp.zeros((), BF16))
                part = part + _group_sum16(hit)
            return part
        part = lax.fori_loop(0, nkb, body, jnp.zeros((16, tq), BF16))
        return jnp.sum(part.astype(F32), axis=0, keepdims=True)

    def key16_value(k16):
        pat = jnp.where(k16 >= 0, k16, k16 ^ jnp.int32(0x7FFF)) & jnp.int32(0xFFFF)
        pat = jnp.where(jnp.logical_and(pat > 0, pat < 0x80), jnp.int32(0x80), pat)
        return pltpu.bitcast(lax.shift_left(pat, jnp.int32(16)), F32).astype(BF16)

    t16 = jnp.where(count16(jnp.zeros(shape, BF16)) >= float(topk), jnp.int32(0), jnp.int32(-32768))

    def bit16_body(b, t):
        cand = t + lax.shift_left(jnp.int32(1), jnp.int32(14) - b)
        return jnp.where(count16(key16_value(cand)) >= float(topk), cand, t)

    t16 = lax.fori_loop(0, 15, bit16_body, jnp.broadcast_to(t16, shape))

    def count(pred):
        def body(j, part):
            base = pl.multiple_of(j * kb, kb)
            for r in range(kb // sub):
                lo = pl.multiple_of(base + r * sub, sub)
                hit = jnp.where(pred(keys[pl.ds(lo, sub), :], lo + srow), 1.0, 0.0)
                part = part + _group_sum(hit)
            return part
        part = lax.fori_loop(0, nkb, body, jnp.zeros((8, tq), F32))
        return jnp.sum(part, axis=0, keepdims=True)

    def bit_body(b, t):
        cand = t + lax.shift_left(jnp.int32(1), jnp.int32(15) - b)
        return jnp.where(count(lambda k, pos: k >= cand) >= float(topk), cand, t)

    thr = lax.fori_loop(0, 16, bit_body, lax.shift_left(t16, jnp.int32(16)))
    thr = jnp.maximum(thr, jnp.int32(INT_MIN + 1))

    def write_bias(sel):
        def body(j, carry):
            base = pl.multiple_of(j * kb, kb)
            bias[pl.ds(base, kb), :] = jnp.where(sel(keys[pl.ds(base, kb), :], base + krow), 0.0, NEG_BIAS)
            return carry
        lax.fori_loop(0, nkb, body, 0)

    n_ge = count(lambda k, pos: k >= thr)
    has_ties = jnp.max(n_ge) > float(topk)

    @pl.when(jnp.logical_not(has_ties))
    def _():
        write_bias(lambda k, pos: k >= thr)

    @pl.when(has_ties)
    def _():
        need = float(topk) - count(lambda k, pos: k > thr)
        nbits = int(np.ceil(np.log2(seq))) + 1

        def bit_body(b, p):
            cand = p + lax.shift_left(jnp.int32(1), jnp.int32(nbits - 1) - b)
            below = count(lambda k, pos: jnp.logical_and(k == thr, pos < cand))
            return jnp.where(below < need, cand, p)

        last = lax.fori_loop(0, nbits, bit_body, jnp.zeros(shape, I32))
        write_bias(lambda k, pos: jnp.logical_or(k > thr, jnp.logical_and(k == thr, pos <= last)))

    ngrp = qn.shape[0]
    stack = N_HEADS // ngrp
    for u in range(ngrp):
        for g in range(stack):
            hh = stack * u + g
            qn[u, g * tq:(g + 1) * tq, :] = q_ref[0, :, hh * HEAD_DIM:(hh + 1) * HEAD_DIM]
    acc_ref[...] = jnp.zeros(acc_ref.shape, F32)

    def att_body(j, ms):
        base = pl.multiple_of(j * kb, kb)
        bb = bias[pl.ds(base, kb), :]
        if stack > 1:
            bb = jnp.concatenate([bb] * stack, axis=1)
        new_m = []
        for u in range(ngrp):
            n = (u * stack) // GQA
            s = _dot_t(k_ref[0, pl.ds(base, kb), n * HEAD_DIM:(n + 1) * HEAD_DIM], qn[u]) + bb
            m_new = jnp.maximum(ms[u], jnp.max(s, axis=0, keepdims=True))
            p = jnp.exp2(s - m_new)
            alpha = jnp.exp2(ms[u] - m_new)
            acc_ref[u] = alpha * acc_ref[u] + _dot(vt_ref[0, n, :, pl.ds(base, kb)], p.astype(BF16))
            new_m.append(m_new)
        return tuple(new_m)

    m0 = tuple(jnp.full((1, stack * tq), -jnp.inf, F32) for _ in range(ngrp))
    lax.fori_loop(0, nkb, att_body, m0)
    for u in range(ngrp):
        o = acc_ref[u, 0:HEAD_DIM, :] / acc_ref[u, HEAD_DIM:HEAD_DIM + 1, :]
        for g in range(stack):
            hh = stack * u + g
            o_ref[0, :, hh * HEAD_DIM:(hh + 1) * HEAD_DIM] = o[:, g * tq:(g + 1) * tq].T.astype(BF16)


def _dsa_prompt_t(q, qi, kw, kib, kb16, vt, nb, seq):
    tq = min(QUERY_BLOCK, seq)
    kb = min(KEY_BLOCK, seq)
    topk = min(TOPK_MAX, seq // 4)
    r3 = lambda a: a.reshape(nb, seq, a.shape[-1])
    qblk = lambda w: pl.BlockSpec((1, tq, w), lambda b, i: (b, i, 0))
    full = lambda w: pl.BlockSpec((1, seq, w), lambda b, i: (b, 0, 0))
    out = pl.pallas_call(
        functools.partial(_dsa_prompt_t_kernel, topk, kb),
        grid=(nb, seq // tq),
        in_specs=[qblk(N_HEADS * HEAD_DIM), qblk(IDX_HEADS * LANES), qblk(LANES), full(LANES), full(KV_WIDTH),
                  pl.BlockSpec((1, N_KV, VT_ROWS, seq), lambda b, i: (b, 0, 0, 0))],
        out_specs=qblk(N_HEADS * HEAD_DIM),
        out_shape=jax.ShapeDtypeStruct((nb, seq, N_HEADS * HEAD_DIM), BF16),
        scratch_shapes=[pltpu.VMEM((IDX_HEADS * tq, LANES), BF16), pltpu.VMEM((seq, tq), I32),
                        pltpu.VMEM((seq, tq), BF16), pltpu.VMEM((seq, tq), F32),
                        pltpu.VMEM((N_HEADS // HEAD_STACK, HEAD_STACK * tq, HEAD_DIM), BF16),
                        pltpu.VMEM((N_HEADS // HEAD_STACK, VT_ROWS, HEAD_STACK * tq), F32)],
        compiler_params=_cparams(("parallel", "arbitrary")),
        name="dsa_prompt",
    )(r3(q), r3(qi), r3(kw), r3(kib), r3(kb16), vt)
    return out.reshape(nb * seq, N_HEADS * HEAD_DIM)


def _dsa_select_kernel(layer, topk, npages, pt_ref, qi_ref, kw_ref, kwb_ref, cik_ref, idx_ref, ibuf, keys, grank, isem):
    b = pl.program_id(0)
    nrow = keys.shape[0]
    group = 8

    def idx_copy(j):
        return pltpu.make_async_copy(cik_ref.at[layer, pt_ref[b, j]], ibuf.at[j], isem)

    def start_idx(j, c):
        idx_copy(j).start()
        return c

    lax.fori_loop(0, npages, start_idx, 0)

    qif = qi_ref[0].astype(F32)
    qi8f = jnp.concatenate([qif[:, h * LANES:h * LANES + IDX_DIM] for h in range(IDX_HEADS)], axis=0)
    qi8 = qi8f.astype(BF16)
    kwrow = kw_ref[0]
    eye = (lax.broadcasted_iota(I32, (IDX_HEADS, LANES), 0) + IDX_DIM
           == lax.broadcasted_iota(I32, (IDX_HEADS, LANES), 1))
    wcol = jnp.sum(jnp.where(eye, jnp.broadcast_to(kwrow, (IDX_HEADS, LANES)), 0.0), axis=1, keepdims=True)

    def wait_idx(j, c):
        idx_copy(j).wait()
        return c

    lax.fori_loop(0, npages, wait_idx, 0)

    def score_body(jj, c):
        rows = []
        for u in range(group):
            lg = _dot(qi8, ibuf[jj * group + u].astype(BF16))
            rows.append(jnp.sum(wcol * jnp.maximum(lg, 0.0), axis=0, keepdims=True))
        tile = jnp.concatenate(rows, axis=0)
        keys[pl.ds(pl.multiple_of(jj * group, group), group), :] = _float_key(tile + 0.0)
        return c

    lax.fori_loop(0, npages // group, score_body, 0)
    lgs = jnp.sum(qi8f * kwb_ref[0].astype(F32)[:, :IDX_DIM], axis=1, keepdims=True)
    scs = jnp.sum(wcol * jnp.maximum(lgs, 0.0), axis=0, keepdims=True)
    tshape = (nrow - npages, LANES)
    first = jnp.logical_and(lax.broadcasted_iota(I32, tshape, 0) == 0, lax.broadcasted_iota(I32, tshape, 1) == 0)
    keys[npages:nrow, :] = jnp.where(first, _float_key(jnp.broadcast_to(scs, tshape) + 0.0), jnp.int32(INT_MIN))

    kall = keys[...]
    pos = lax.broadcasted_iota(I32, (nrow, LANES), 0) * LANES + lax.broadcasted_iota(I32, (nrow, LANES), 1)
    shape = (1, 1)

    def count(pred):
        hit = jnp.where(pred, 1.0, 0.0)
        return jnp.sum(jnp.sum(hit, axis=1, keepdims=True), axis=0, keepdims=True)

    thr = _topk_threshold(lambda cand: count(kall >= cand), float(topk), shape)
    thr = jnp.maximum(thr, jnp.int32(INT_MIN + 1))
    need = float(topk) - count(kall > thr)
    tied = kall == thr
    nbits = int(np.ceil(np.log2(nrow * LANES))) + 1

    def bit_body(bb, p):
        cand = p + lax.shift_left(jnp.int32(1), jnp.int32(nbits - 1) - bb)
        below = count(jnp.logical_and(tied, pos < cand))
        return jnp.where(below < need, cand, p)

    n_gt = float(topk) - need
    n_tied = count(tied)
    surplus = jnp.max(n_gt + n_tied) > float(topk)
    last = lax.cond(surplus,
                    lambda: lax.fori_loop(0, nbits, bit_body, jnp.zeros(shape, I32)),
                    lambda: jnp.full(shape, nrow * LANES, I32))
    sel = jnp.logical_or(kall > thr, jnp.logical_and(tied, pos <= last))
    msk = jnp.where(sel, 1.0, 0.0)

    upper = (lax.broadcasted_iota(I32, (LANES, LANES), 0) <= lax.broadcasted_iota(I32, (LANES, LANES), 1))
    within = _dot(msk.astype(BF16), jnp.where(upper, 1.0, 0.0).astype(BF16))
    rowsum = jnp.broadcast_to(jnp.sum(msk, axis=1, keepdims=True), (nrow, LANES))
    before = (lax.broadcasted_iota(I32, (nrow, nrow), 1) < lax.broadcasted_iota(I32, (nrow, nrow), 0))
    offset = _dot(jnp.where(before, 1.0, 0.0).astype(BF16), rowsum.astype(BF16))
    grank[...] = jnp.where(sel, offset + within - 1.0, -1.0)

    slot = lax.broadcasted_iota(I32, (topk, LANES), 0).astype(F32)
    lane_f = lax.broadcasted_iota(I32, (1, LANES), 1).astype(F32)

    def compact_body(r, acc):
        g = jnp.broadcast_to(grank[pl.ds(r, 1), :], (topk, LANES))
        flat = jnp.broadcast_to(lax.convert_element_type(r * LANES, F32) + lane_f, (topk, LANES))
        return acc + jnp.where(g == slot, flat, 0.0)

    acc = lax.fori_loop(0, npages + 1, compact_body, jnp.zeros((topk, LANES), F32))
    idx_ref[0] = jnp.sum(acc, axis=1, keepdims=True).astype(I32)


def _dsa_attend_kernel(layer, topk, npages, pt_ref, ix_ref, q_ref, kn_ref, vn_ref, ck_ref, cv_ref, o_ref,
                       kg, vg, ksem, vsem):
    b = pl.program_id(0)
    past = npages * PAGE

    def issue(t, c):
        ix = ix_ref[b, t]

        @pl.when(ix < past)
        def _():
            pg = pt_ref[b, lax.shift_right_logical(ix, 7)]
            off = jnp.bitwise_and(ix, PAGE - 1)
            pltpu.make_async_copy(ck_ref.at[layer, pg, off], kg.at[t], ksem).start()
            pltpu.make_async_copy(cv_ref.at[layer, pg, off], vg.at[t], vsem).start()

        @pl.when(ix >= past)
        def _():
            pltpu.make_async_copy(kn_ref.at[b], kg.at[t], ksem).start()
            pltpu.make_async_copy(vn_ref.at[b], vg.at[t], vsem).start()

        return c

    lax.fori_loop(0, topk, issue, 0, unroll=4)

    def wait(t, c):
        pltpu.make_async_copy(kn_ref.at[b], kg.at[t], ksem).wait()
        pltpu.make_async_copy(vn_ref.at[b], vg.at[t], vsem).wait()
        return c

    lax.fori_loop(0, topk, wait, 0, unroll=8)

    qf = q_ref[0].astype(F32)
    pad = jnp.zeros((8 - GQA, HEAD_DIM), F32)
    for n in range(N_KV):
        rows = [qf[:, (GQA * n + g) * HEAD_DIM:(GQA * n + g + 1) * HEAD_DIM] for g in range(GQA)]
        q8 = jnp.concatenate(rows + [pad], axis=0).astype(BF16)
        s = _dot_t(q8, kg[:, n, :].astype(BF16))
        p = jnp.exp2(s - jnp.max(s, axis=1, keepdims=True))
        p = p / jnp.sum(p, axis=1, keepdims=True)
        o = _dot(p.astype(BF16), vg[:, n, :].astype(BF16))
        for g in range(GQA):
            hh = GQA * n + g
            o_ref[0, :, hh * HEAD_DIM:(hh + 1) * HEAD_DIM] = o[g:g + 1, :].astype(BF16)


def _dsa_sample(layer, q, qi, kw, kwb, kf, vf, cache_idx_k, cache_k, cache_v, page_table):
    nb, npages = page_table.shape
    assert npages % 8 == 0 and PAGE == 128
    topk = min(TOPK_MAX, (npages * PAGE + 1) // 4)
    nrow = -(-(npages + 1) // LANES) * LANES
    r3 = lambda a: a.reshape(nb, 1, a.shape[-1])
    anyspec = pl.BlockSpec(memory_space=pl.ANY)
    blk1 = lambda w: pl.BlockSpec((1, 1, w), lambda b, pt: (b, 0, 0))
    idx = pl.pallas_call(
        functools.partial(_dsa_select_kernel, layer, topk, npages),
        grid_spec=pltpu.PrefetchScalarGridSpec(
            num_scalar_prefetch=1,
            grid=(nb,),
            in_specs=[blk1(IDX_HEADS * LANES), blk1(LANES), blk1(LANES), anyspec],
            out_specs=pl.BlockSpec((1, topk, 1), lambda b, pt: (b, 0, 0)),
            scratch_shapes=[pltpu.VMEM((npages, IDX_DIM, PAGE), F32), pltpu.VMEM((nrow, LANES), I32),
                            pltpu.VMEM((nrow, LANES), F32), pltpu.SemaphoreType.DMA(())],
        ),
        out_shape=jax.ShapeDtypeStruct((nb, topk, 1), I32),
        compiler_params=_cparams(("arbitrary",)),
        name="dsa_select",
    )(page_table, r3(qi), r3(kw), r3(kwb), jnp.swapaxes(cache_idx_k, 2, 3))
    blk2 = lambda w: pl.BlockSpec((1, 1, w), lambda b, pt, ix: (b, 0, 0))
    rowshape = (topk, N_KV, HEAD_DIM)
    out = pl.pallas_call(
        functools.partial(_dsa_attend_kernel, layer, topk, npages),
        grid_spec=pltpu.PrefetchScalarGridSpec(
            num_scalar_prefetch=2,
            grid=(nb,),
            in_specs=[blk2(N_HEADS * HEAD_DIM), anyspec, anyspec, anyspec, anyspec],
            out_specs=blk2(N_HEADS * HEAD_DIM),
            scratch_shapes=[pltpu.VMEM(rowshape, F32), pltpu.VMEM(rowshape, F32),
                            pltpu.SemaphoreType.DMA(()), pltpu.SemaphoreType.DMA(())],
        ),
        out_shape=jax.ShapeDtypeStruct((nb, 1, N_HEADS * HEAD_DIM), BF16),
        compiler_params=_cparams(("arbitrary",)),
        name="dsa_attend",
    )(page_table, idx.reshape(nb, topk), r3(q), kf.reshape(nb, N_KV, HEAD_DIM), vf.reshape(nb, N_KV, HEAD_DIM),
      cache_k, cache_v)
    return out.reshape(nb, N_HEADS * HEAD_DIM)


def _mix_body(a, o, sga, sgb, x, wba, wbb, wout, g_ref):
    merged = sga.astype(F32) * _dot(a, wba[...]) + sgb.astype(F32) * _dot(o, wbb[...])
    mix = _dot(merged.astype(BF16), wout[...])
    return x + _rms(mix, g_ref[...])


def _mix_out_kernel(a_ref, o_ref, sga_ref, sgb_ref, x_ref, wba, wbb, wout, g_ref, y_ref):
    y_ref[...] = _mix_body(a_ref[...], o_ref[...], sga_ref[...], sgb_ref[...], x_ref[...], wba, wbb, wout, g_ref)


def _mix_out(a, o, sga, sgb, x2d, lw):
    m = x2d.shape[0]
    tm = min(ROW_TILE, m)
    row = lambda w: pl.BlockSpec((tm, w), lambda i: (i, 0))
    return pl.pallas_call(
        _mix_out_kernel,
        grid=(m // tm,),
        in_specs=[row(A_WIDTH), row(D_MODEL), row(D_MODEL), row(D_MODEL), row(D_MODEL),
                  _const_spec((A_WIDTH, D_MODEL)), _const_spec((D_MODEL, D_MODEL)), _const_spec((D_MODEL, D_MODEL)),
                  _const_spec((1, D_MODEL))],
        out_specs=row(D_MODEL),
        out_shape=jax.ShapeDtypeStruct((m, D_MODEL), F32),
        compiler_params=_cparams(("parallel",)),
        name="mix_out",
    )(a, o, sga, sgb, x2d, lw["wba"], lw["wbb"], lw["wout"], lw["g1"])


def _mix_cross_prompt_kernel(a_ref, o_ref, sga_ref, sgb_ref, x_ref, wba, wbb, wout, g1, g2, g3, wq, wo,
                             mk_ref, mv_ref, y_ref):
    x = _mix_body(a_ref[0], o_ref[0], sga_ref[0], sgb_ref[0], x_ref[0], wba, wbb, wout, g1)
    h = _rms(x, g2[...]).astype(BF16)
    qm = (_dot(h, wq[...]) * (MEM_HEAD_DIM ** -0.5)).astype(BF16)
    outs = []
    for hh in range(MEM_HEADS):
        cs = slice(hh * MEM_HEAD_DIM, (hh + 1) * MEM_HEAD_DIM)
        s = _dot_t(qm[:, cs], mk_ref[0, :, cs])
        p = jnp.exp(s - jnp.max(s, axis=1, keepdims=True))
        p = p / jnp.sum(p, axis=1, keepdims=True)
        outs.append(_dot(p.astype(BF16), mv_ref[0, :, cs]).astype(BF16))
    c = _dot(jnp.concatenate(outs, axis=1), wo[...])
    y_ref[0] = x + _rms(c, g3[...])


def _mix_cross_prompt(a, o, sga, sgb, x2d, mk, mv, lw, nb, seq):
    tm = min(ROW_TILE, seq)
    mt = mk.shape[0] // nb
    blk = lambda w: pl.BlockSpec((1, tm, w), lambda b, i: (b, i, 0))
    mb = pl.BlockSpec((1, mt, MEM_WIDTH), lambda b, i: (b, 0, 0))
    r3 = lambda t: t.reshape(nb, seq, t.shape[-1])
    vec = _const_spec((1, D_MODEL))
    out = pl.pallas_call(
        _mix_cross_prompt_kernel,
        grid=(nb, seq // tm),
        in_specs=[blk(A_WIDTH), blk(D_MODEL), blk(D_MODEL), blk(D_MODEL), blk(D_MODEL),
                  _const_spec((A_WIDTH, D_MODEL)), _const_spec((D_MODEL, D_MODEL)), _const_spec((D_MODEL, D_MODEL)),
                  vec, vec, vec, _const_spec((D_MODEL, MEM_WIDTH)), _const_spec((MEM_WIDTH, D_MODEL)), mb, mb],
        out_specs=blk(D_MODEL),
        out_shape=jax.ShapeDtypeStruct((nb, seq, D_MODEL), F32),
        compiler_params=_cparams(("parallel", "parallel")),
        name="mix_cross_prompt",
    )(r3(a), r3(o), r3(sga), r3(sgb), r3(x2d), lw["wba"], lw["wbb"], lw["wout"], lw["g1"], lw["g2"], lw["g3"],
      lw["wmq"], lw["wmo"], mk.reshape(nb, mt, MEM_WIDTH), mv.reshape(nb, mt, MEM_WIDTH))
    return out.reshape(nb * seq, D_MODEL)


def _cross_sample_kernel(x_ref, g2, g3, wq, wo, mk_ref, mv_ref, y_ref):
    x = x_ref[...]
    bt = x.shape[0]
    h = _rms(x, g2[...]).astype(BF16)
    qm = (_dot(h, wq[...]) * (MEM_HEAD_DIM ** -0.5)).astype(BF16)
    rows = []
    for b in range(bt):
        qb = jnp.broadcast_to(qm[b:b + 1, :], (8, MEM_WIDTH))
        outs = []
        for hh in range(MEM_HEADS):
            cs = slice(hh * MEM_HEAD_DIM, (hh + 1) * MEM_HEAD_DIM)
            s = _dot_t(qb[:, cs], mk_ref[0, b, :, hh, :].astype(BF16))
            p = jnp.exp(s - jnp.max(s, axis=1, keepdims=True))
            p = p / jnp.sum(p, axis=1, keepdims=True)
            outs.append(_dot(p.astype(BF16), mv_ref[0, b, :, hh, :].astype(BF16)))
        rows.append(jnp.concatenate(outs, axis=1)[0:1, :])
    att = jnp.concatenate(rows, axis=0).astype(BF16)
    c = _dot(att, wo[...])
    y_ref[...] = x + _rms(c, g3[...])


def _cross_sample(layer, x2d, mk, mv, lw):
    nb = x2d.shape[0]
    bt = 8
    mt = mk.shape[2]
    xb = pl.BlockSpec((bt, D_MODEL), lambda i: (i, 0))
    mb = pl.BlockSpec((1, bt, mt, MEM_HEADS, MEM_HEAD_DIM), lambda i: (layer, i, 0, 0, 0))
    return pl.pallas_call(
        _cross_sample_kernel,
        grid=(nb // bt,),
        in_specs=[xb, _const_spec((1, D_MODEL)), _const_spec((1, D_MODEL)), _const_spec((D_MODEL, MEM_WIDTH)),
                  _const_spec((MEM_WIDTH, D_MODEL)), mb, mb],
        out_specs=xb,
        out_shape=jax.ShapeDtypeStruct((nb, D_MODEL), F32),
        compiler_params=_cparams(("parallel",)),
        name="cross_sample",
    )(x2d, lw["g2"], lw["g3"], lw["wmq"], lw["wmo"], mk, mv)


def _ffn_seq_kernel(x_ref, g4, g5, wup, cw, cb, wdn, st_ref, y_ref, ns_ref, carry, ext, acc):
    tm = x_ref.shape[1]
    t = pl.program_id(1)

    @pl.when(t == 0)
    def _():
        carry[6:8, :] = st_ref[0]

    x = x_ref[0]
    h = _rms(x, g4[...]).astype(BF16)
    for c in range(D_FF // FF_CHUNK):
        prod = None
        for half in range(2):
            cs = slice(half * D_FF + c * FF_CHUNK, half * D_FF + (c + 1) * FF_CHUNK)
            up = _dot(h, wup[:, cs])
            ext[half, 0:8, :] = carry[:, cs]
            ext[half, 8:8 + tm, :] = up
            carry[:, cs] = up[tm - 8:, :]
            conv = (cb[:, cs] + cw[0:1, cs] * ext[half, 6:6 + tm, :] + cw[1:2, cs] * ext[half, 7:7 + tm, :]
                    + cw[2:3, cs] * up)
            prod = _gelu(conv) if half == 0 else prod * conv
        part = _dot(prod.astype(BF16), wdn[c * FF_CHUNK:(c + 1) * FF_CHUNK, :])
        if c == 0:
            acc[...] = part
        else:
            acc[...] += part
    y_ref[0] = x + _rms(acc[...], g5[...])
    ns_ref[0] = carry[6:8, :]


def _ffn_seq(x2d, state, lw, nb, seq):
    tm = min(ROW_TILE, seq)
    xb = pl.BlockSpec((1, tm, D_MODEL), lambda b, t: (b, t, 0))
    sb = pl.BlockSpec((1, CONV_WIDTH - 1, 2 * D_FF), lambda b, t: (b, 0, 0))
    y, ns = pl.pallas_call(
        _ffn_seq_kernel,
        grid=(nb, seq // tm),
        in_specs=[xb, _const_spec((1, D_MODEL)), _const_spec((1, D_MODEL)), _const_spec((D_MODEL, 2 * D_FF)),
                  _const_spec((CONV_WIDTH, 2 * D_FF)), _const_spec((1, 2 * D_FF)), _const_spec((D_FF, D_MODEL)), sb],
        out_specs=[xb, sb],
        out_shape=[jax.ShapeDtypeStruct((nb, seq, D_MODEL), F32),
                   jax.ShapeDtypeStruct((nb, CONV_WIDTH - 1, 2 * D_FF), F32)],
        scratch_shapes=[pltpu.VMEM((8, 2 * D_FF), F32), pltpu.VMEM((2, tm + 8, FF_CHUNK), F32),
                        pltpu.VMEM((tm, D_MODEL), F32)],
        compiler_params=_cparams(("arbitrary", "arbitrary")),
        name="ffn_seq",
    )(x2d.reshape(nb, seq, D_MODEL), lw["g4"], lw["g5"], lw["wup"], lw["cw"], lw["cb"], lw["wdn"], state)
    return y.reshape(nb * seq, D_MODEL), ns


def _ffn_step_kernel(x_ref, g4, g5, wup, cw, cb, wdn, s0_ref, s1_ref, y_ref, up_ref):
    x = x_ref[...]
    h = _rms(x, g4[...]).astype(BF16)
    acc = None
    for c in range(D_FF // FF_CHUNK):
        prod = None
        for half in range(2):
            cs = slice(half * D_FF + c * FF_CHUNK, half * D_FF + (c + 1) * FF_CHUNK)
            up = _dot(h, wup[:, cs])
            up_ref[:, cs] = up
            conv = cb[:, cs] + cw[0:1, cs] * s0_ref[:, cs] + cw[1:2, cs] * s1_ref[:, cs] + cw[2:3, cs] * up
            prod = _gelu(conv) if half == 0 else prod * conv
        part = _dot(prod.astype(BF16), wdn[c * FF_CHUNK:(c + 1) * FF_CHUNK, :])
        acc = part if acc is None else acc + part
    y_ref[...] = x + _rms(acc, g5[...])


def _ffn_step(x2d, state, lw):
    nb = x2d.shape[0]
    s0 = state[:, 0, :]
    s1 = state[:, 1, :]
    full = lambda w: pl.BlockSpec((nb, w), lambda i: (0, 0))
    y, up = pl.pallas_call(
        _ffn_step_kernel,
        grid=(1,),
        in_specs=[full(D_MODEL), _const_spec((1, D_MODEL)), _const_spec((1, D_MODEL)),
                  _const_spec((D_MODEL, 2 * D_FF)), _const_spec((CONV_WIDTH, 2 * D_FF)), _const_spec((1, 2 * D_FF)),
                  _const_spec((D_FF, D_MODEL)), full(2 * D_FF), full(2 * D_FF)],
        out_specs=[full(D_MODEL), full(2 * D_FF)],
        out_shape=[jax.ShapeDtypeStruct((nb, D_MODEL), F32), jax.ShapeDtypeStruct((nb, 2 * D_FF), F32)],
        compiler_params=_cparams(("arbitrary",)),
        name="ffn_step",
    )(x2d, lw["g4"], lw["g5"], lw["wup"], lw["cw"], lw["cb"], lw["wdn"], s0, s1)
    return y, jnp.stack([s1, up], axis=1)


def _rope_tables(pos):
    def tab(dim, reps):
        half = dim // 2
        inv = ROPE_THETA ** (-jnp.arange(half, dtype=F32) / half)
        ang = pos.astype(F32)[:, None] * inv[None, :]
        cos = jnp.concatenate([jnp.cos(ang)] * 2, axis=1)
        sin = jnp.concatenate([-jnp.sin(ang), jnp.sin(ang)], axis=1)
        return jnp.tile(cos, (1, reps)), jnp.tile(sin, (1, reps))
    c128, s128 = tab(HEAD_DIM, 1)
    c64, s64 = tab(IDX_DIM, LANES // IDX_DIM)
    return c128, s128, c64, s64


def _layer_weights(l, norms, w_in, ln_v_g, ln_v_b, w_spatial, b_spatial, w_branch_a, w_branch_b, w_out, w_mem_q,
                   w_mem_kv, w_mem_o, w_up, conv_w, conv_b, w_down):
    offs = np.cumsum((0,) + IN_SIZES)
    seg = lambda i: w_in[l][:, offs[i]:offs[i + 1]].astype(BF16)
    wkw = jnp.concatenate([w_in[l][:, offs[6]:offs[8]],
                           jnp.zeros((D_MODEL, LANES - IDX_DIM - IDX_HEADS), F32)], axis=1).astype(BF16)
    lw = dict(
        wu=seg(0), wv=seg(1), wq=seg(2), wk=seg(3), wvv=seg(4), wqi=seg(5), wkw=wkw, wga=seg(8), wgb=seg(9),
        ln_g=ln_v_g[l][None, :], ln_b=ln_v_b[l][None, :],
        sp_w=w_spatial[l], sp_b=jnp.repeat(b_spatial[l].T, A_GROUP_DIM, axis=1),
        sp_w1=jnp.repeat(w_spatial[l][:, 0, 0], A_GROUP_DIM)[None, :],
        sp_b1=jnp.repeat(b_spatial[l][:, 0], A_GROUP_DIM)[None, :],
        wba=w_branch_a[l].astype(BF16), wbb=w_branch_b[l].astype(BF16), wout=w_out[l].astype(BF16),
        wmq=w_mem_q[l].astype(BF16), wmkv=w_mem_kv[l].astype(BF16), wmo=w_mem_o[l].astype(BF16),
        wup=w_up[l].astype(BF16), cw=conv_w[l], cb=conv_b[l][None, :], wdn=w_down[l].astype(BF16),
    )
    for i in range(6):
        lw["g%d" % i] = norms[l, i][None, :]
    return lw


def kernel(x_prompt, x_sample, cache_k, cache_v, cache_idx_k, cache_mem_k, cache_mem_v, state_conv, page_table,
           mem_prompt, norms, w_in, ln_v_g, ln_v_b, w_spatial, b_spatial, w_branch_a, w_branch_b, w_out,
           w_mem_q, w_mem_kv, w_mem_o, w_up, conv_w, conv_b, w_down):
    nb, seq, _ = x_prompt.shape
    nd, tdec, _ = x_sample.shape
    assert tdec == 1 and seq % CHUNK == 0
    depth = norms.shape[0]
    mt = mem_prompt.shape[1]
    past = page_table.shape[1] * PAGE

    tabs_p = _rope_tables(jnp.arange(seq))
    tabs_s = tuple(jnp.broadcast_to(t, (nd, LANES)) for t in _rope_tables(jnp.full((1,), past)))
    yp = x_prompt.reshape(nb * seq, D_MODEL)
    ys = x_sample.reshape(nd, D_MODEL)
    mem2d = mem_prompt.reshape(nb * mt, D_MODEL)
    zero_state = jnp.zeros((nb, CONV_WIDTH - 1, 2 * D_FF), F32)

    outs = [[] for _ in range(12)]
    kp_all = vp_all = None
    for l in range(depth):
        lw = _layer_weights(l, norms, w_in, ln_v_g, ln_v_b, w_spatial, b_spatial, w_branch_a, w_branch_b, w_out,
                            w_mem_q, w_mem_kv, w_mem_o, w_up, conv_w, conv_b, w_down)
        mkf, mvf, mkb, mvb = _memkv(mem2d, lw["wmkv"])
        a, vrow, q, kp_all, kb16, vp_all, vt16, qi, kw, kwb, sga, sgb = _in_proj(
            yp, lw, tabs_p, seq, CHUNK, (l, kp_all, vp_all))
        o = _dsa_prompt_t(q, qi, kw, kwb, kb16, vt16, nb, seq)
        yp = _mix_cross_prompt(a, o, sga, sgb, yp, mkb, mvb, lw, nb, seq)
        yp, cp = _ffn_seq(yp, zero_state, lw, nb, seq)
        outs[2].append(kw[:, :IDX_DIM].reshape(nb, seq, IDX_DIM))
        outs[6].append(mkf.reshape(nb, mt, MEM_HEADS, MEM_HEAD_DIM))
        outs[7].append(mvf.reshape(nb, mt, MEM_HEADS, MEM_HEAD_DIM))
        outs[8].append(vrow)
        outs[10].append(cp)
        a, vrow, q, kf, kb16, vf, vb16, qi, kw, kwb, sga, sgb = _in_proj(ys, lw, tabs_s, 1, 1)
        o = _dsa_sample(l, q, qi, kw, kwb, kf, vf, cache_idx_k, cache_k, cache_v, page_table)
        ys = _mix_out(a, o, sga, sgb, ys, lw)
        ys = _cross_sample(l, ys, cache_mem_k, cache_mem_v, lw)
        ys, cs = _ffn_step(ys, state_conv[l], lw)
        outs[3].append(kf.reshape(nd, 1, N_KV, HEAD_DIM))
        outs[4].append(vf.reshape(nd, 1, N_KV, HEAD_DIM))
        outs[5].append(kw[:, :IDX_DIM].reshape(nd, 1, IDX_DIM))
        outs[9].append(vrow.reshape(nd, 1, A_WIDTH))
        outs[11].append(cs)
    stacked = [jnp.stack(o) if o else None for o in outs]
    stacked[0], stacked[1] = kp_all, vp_all
    return (yp.reshape(nb, seq, D_MODEL), ys.reshape(nd, 1, D_MODEL)) + tuple(stacked)
```
